```python
import math
import jax, jax.numpy as jnp
from jax import lax
import numpy as np

D_MODEL = 2048
BATCH = 2
SEQ = 16384
DEPTH = 1

D_MIX = D_MODEL
HEAD_DIM = 128
N_Q_HEADS = 8
N_KV_HEADS = 2
GROUP = N_Q_HEADS // N_KV_HEADS
ATTN_W = N_Q_HEADS * HEAD_DIM
LRU_W = D_MIX - ATTN_W
LRU_BLOCKS = 8
LRU_BW = LRU_W // LRU_BLOCKS
LRU_C = 8.0
CONV_W = 4
CONV_LEFT = 2
D_FF = 5632
PLE_DIM = 256
GRID_W = 64
ROPE_THETA = 10000.0
AXIS_DIM = HEAD_DIM // 2
Q_BLOCK = 128
EPS = 1e-6
IN_COLS = ATTN_W + 2 * N_KV_HEADS * HEAD_DIM + 2 * LRU_W

kernel_name = "hymba_style_bidir_attn_rglru_macaron_layer"


def rms_norm(x, g):
    xf = x.astype(jnp.float32)
    y = xf * lax.rsqrt(jnp.mean(xf * xf, axis=-1, keepdims=True) + EPS)
    return (y * g.astype(jnp.float32)).astype(x.dtype)


def swiglu(h, w1, w3, w2):
    return (jax.nn.silu(h @ w1) * (h @ w3)) @ w2


def axial_rope_tables(seq_len, dtype):
    rows = seq_len // GRID_W
    r = jnp.repeat(jnp.arange(rows, dtype=jnp.float32), GRID_W)
    c = jnp.tile(jnp.arange(GRID_W, dtype=jnp.float32), rows)
    inv = ROPE_THETA ** (-jnp.arange(0, AXIS_DIM, 2, dtype=jnp.float32) / AXIS_DIM)
    ang = jnp.concatenate([r[:, None] * inv, c[:, None] * inv], axis=-1)
    return jnp.cos(ang).astype(dtype), jnp.sin(ang).astype(dtype)


def apply_rope(x, cos, sin):
    x1 = x[..., 0::2]
    x2 = x[..., 1::2]
    c = cos[None, :, None, :]
    s = sin[None, :, None, :]
    out = jnp.stack([x1 * c - x2 * s, x1 * s + x2 * c], axis=-1)
    return out.reshape(x.shape)


def bidir_gqa(q, k, v):
    b, s = q.shape[0], q.shape[1]
    nb = s // Q_BLOCK
    scale = 1.0 / math.sqrt(HEAD_DIM)
    qb = q.reshape(b, nb, Q_BLOCK, N_KV_HEADS, GROUP, HEAD_DIM).transpose(1, 0, 2, 3, 4, 5)

    def one_block(qblk):
        sc = jnp.einsum('bqkgd,bskd->bkgqs', qblk, k).astype(jnp.float32) * scale
        pr = jax.nn.softmax(sc, axis=-1).astype(v.dtype)
        return jnp.einsum('bkgqs,bskd->bqkgd', pr, v)

    o = lax.map(one_block, qb)
    return o.transpose(1, 0, 2, 3, 4, 5).reshape(b, s, ATTN_W)


def centred_dw_conv(u, w, bias):
    s = u.shape[1]
    up = jnp.pad(u, ((0, 0), (CONV_LEFT, CONV_W - 1 - CONV_LEFT), (0, 0)))
    out = bias
    for j in range(CONV_W):
        out = out + up[:, j:j + s] * w[j]
    return out


def _lin_combine(e1, e2):
    a1, b1 = e1
    a2, b2 = e2
    return a1 * a2, a2 * b1 + b2


def rglru(u, w_a, b_a, w_i, b_i, lam, reverse):
    b, s, _ = u.shape
    ub = u.reshape(b, s, LRU_BLOCKS, LRU_BW)
    r = jax.nn.sigmoid((jnp.einsum('bshi,hij->bshj', ub, w_a) + b_a).astype(jnp.float32)).reshape(b, s, LRU_W)
    i = jax.nn.sigmoid((jnp.einsum('bshi,hij->bshj', ub, w_i) + b_i).astype(jnp.float32)).reshape(b, s, LRU_W)
    log_a = -LRU_C * jax.nn.softplus(-lam.astype(jnp.float32)) * r
    a = jnp.exp(log_a)
    mult = jnp.sqrt(-jnp.expm1(2.0 * log_a))
    bx = mult * i * u.astype(jnp.float32)
    _, h = lax.associative_scan(_lin_combine, (a, bx), axis=1, reverse=reverse)
    return h.astype(u.dtype)


def setup_inputs(seed: int = 0) -> dict:
    key = jax.random.key(seed)
    ks = jax.random.split(key, 32)
    f32 = jnp.float32

    def w(k, shape, fan_in, gain=1.0):
        return jax.random.normal(k, shape, f32) * (gain * fan_in ** -0.5)

    def gain(k, shape):
        return jnp.ones(shape, f32) + 0.01 * jax.random.normal(k, shape, f32)

    a0 = jax.random.uniform(ks[20], (DEPTH, 2, LRU_W), f32, 0.9, 0.999)
    sg = a0 ** (1.0 / LRU_C)
    lam = jnp.log(sg) - jnp.log1p(-sg)

    return {
        "x": jax.random.normal(ks[0], (BATCH, SEQ, D_MODEL), f32),
        "p": jax.random.normal(ks[1], (DEPTH, BATCH, SEQ, PLE_DIM), f32),
        "norm_ffn1": gain(ks[2], (DEPTH, D_MODEL)),
        "w1_ffn1": w(ks[3], (DEPTH, D_MODEL, D_FF), D_MODEL),
        "w3_ffn1": w(ks[4], (DEPTH, D_MODEL, D_FF), D_MODEL),
        "w2_ffn1": w(ks[5], (DEPTH, D_FF, D_MODEL), D_FF),
        "norm_mix": gain(ks[6], (DEPTH, D_MODEL)),
        "w_in": w(ks[7], (DEPTH, D_MODEL, IN_COLS), D_MODEL),
        "q_norm": gain(ks[8], (DEPTH, HEAD_DIM)),
        "k_norm": gain(ks[9], (DEPTH, HEAD_DIM)),
        "conv_w": w(ks[10], (DEPTH, CONV_W, LRU_W), CONV_W),
        "conv_b": 0.01 * jax.random.normal(ks[11], (DEPTH, LRU_W), f32),
        "lru_wa": w(ks[12], (DEPTH, 2, LRU_BLOCKS, LRU_BW, LRU_BW), LRU_BW),
        "lru_ba": 0.01 * jax.random.normal(ks[13], (DEPTH, 2, LRU_BLOCKS, LRU_BW), f32),
        "lru_wi": w(ks[14], (DEPTH, 2, LRU_BLOCKS, LRU_BW, LRU_BW), LRU_BW),
        "lru_bi": 0.01 * jax.random.normal(ks[15], (DEPTH, 2, LRU_BLOCKS, LRU_BW), f32),
        "lru_lambda": lam,
        "w_out": w(ks[16], (DEPTH, D_MIX, D_MODEL), D_MIX),
        "norm_ffn2": gain(ks[17], (DEPTH, D_MODEL)),
        "w1_ffn2": w(ks[18], (DEPTH, D_MODEL, D_FF), D_MODEL),
        "w3_ffn2": w(ks[19], (DEPTH, D_MODEL, D_FF), D_MODEL),
        "w2_ffn2": w(ks[21], (DEPTH, D_FF, D_MODEL), D_FF),
        "norm_ple": gain(ks[22], (DEPTH, D_MODEL)),
        "w_ple_gate": w(ks[23], (DEPTH, D_MODEL, D_MODEL), D_MODEL),
        "w_ple_proj": w(ks[24], (DEPTH, PLE_DIM, D_MODEL), PLE_DIM),
        "norm_final": gain(ks[25], (D_MODEL,)),
    }


def reference(x, p, norm_ffn1, w1_ffn1, w3_ffn1, w2_ffn1, norm_mix, w_in, q_norm, k_norm,
              conv_w, conv_b, lru_wa, lru_ba, lru_wi, lru_bi, lru_lambda, w_out,
              norm_ffn2, w1_ffn2, w3_ffn2, w2_ffn2, norm_ple, w_ple_gate, w_ple_proj,
              norm_final):
    b, s, _ = x.shape
    cos, sin = axial_rope_tables(s, x.dtype)
    kv_w = N_KV_HEADS * HEAD_DIM
    for l in range(DEPTH):
        x = x + 0.5 * swiglu(rms_norm(x, norm_ffn1[l]), w1_ffn1[l], w3_ffn1[l], w2_ffn1[l])

        h = rms_norm(x, norm_mix[l])
        proj = h @ w_in[l]
        q = proj[..., :ATTN_W].reshape(b, s, N_Q_HEADS, HEAD_DIM)
        k = proj[..., ATTN_W:ATTN_W + kv_w].reshape(b, s, N_KV_HEADS, HEAD_DIM)
        v = proj[..., ATTN_W + kv_w:ATTN_W + 2 * kv_w].reshape(b, s, N_KV_HEADS, HEAD_DIM)
        u = proj[..., ATTN_W + 2 * kv_w:ATTN_W + 2 * kv_w + LRU_W]
        y = proj[..., ATTN_W + 2 * kv_w + LRU_W:]

        q = apply_rope(rms_norm(q, q_norm[l]), cos, sin)
        k = apply_rope(rms_norm(k, k_norm[l]), cos, sin)
        attn_out = bidir_gqa(q, k, v)

        uc = centred_dw_conv(u, conv_w[l], conv_b[l])
        h_f = rglru(uc, lru_wa[l, 0], lru_ba[l, 0], lru_wi[l, 0], lru_bi[l, 0], lru_lambda[l, 0], False)
        h_b = rglru(uc, lru_wa[l, 1], lru_ba[l, 1], lru_wi[l, 1], lru_bi[l, 1], lru_lambda[l, 1], True)
        lru_out = (h_f + h_b) * jax.nn.gelu(y)

        mixed = jnp.concatenate([attn_out, lru_out], axis=-1)
        x = x + mixed @ w_out[l]

        x = x + 0.5 * swiglu(rms_norm(x, norm_ffn2[l]), w1_ffn2[l], w3_ffn2[l], w2_ffn2[l])

        gate = jax.nn.sigmoid(rms_norm(x, norm_ple[l]) @ w_ple_gate[l])
        x = x + gate * (p[l] @ w_ple_proj[l])
    return rms_norm(x, norm_final)
```

```python
import functools
import math

import jax
import jax.numpy as jnp
from jax import lax
from jax.experimental import pallas as pl
from jax.experimental.pallas import tpu as pltpu

F32 = jnp.float32
BF16 = jnp.bfloat16

EPS = 1e-6
HEAD_DIM = 128
N_Q_HEADS = 8
N_KV_HEADS = 2
GROUP = N_Q_HEADS // N_KV_HEADS
ATTN_W = N_Q_HEADS * HEAD_DIM
KV_W = N_KV_HEADS * HEAD_DIM
LRU_BLOCKS = 8
LRU_BW = 128
LRU_W = LRU_BLOCKS * LRU_BW
LRU_C = 8.0
CONV_W = 4
CONV_LEFT = 2
GRID_W = 64
ROPE_THETA = 10000.0
AXIS_DIM = HEAD_DIM // 2

V7X_VMEM_BYTES = 64 * 1024 * 1024
VMEM_LIMIT_BYTES = V7X_VMEM_BYTES - 8 * 1024 * 1024
SUBLANES = 8
Q_SCALE = math.log2(math.e) / math.sqrt(HEAD_DIM)
NEG_BIG = -1e30


def _tile(n, pref):
    t = pref
    while n % t:
        t //= 2
    return t


def _params(sem):
    return pltpu.CompilerParams(dimension_semantics=sem, vmem_limit_bytes=VMEM_LIMIT_BYTES)


def _rms(x, g):
    return x * lax.rsqrt(jnp.mean(x * x, axis=-1, keepdims=True) + EPS) * g


def _resident(shape):
    nd = len(shape)
    return pl.BlockSpec(shape, lambda *_: (0,) * nd, pipeline_mode=pl.Buffered(1))


def _ffn_body(x_ref, g_ref, w1_ref, w3_ref, w2_ref, o_ref, h_ref, *, row_chunk):
    tm = h_ref.shape[0]
    chunks = [pl.ds(r * row_chunk, row_chunk) for r in range(tm // row_chunk)]

    @pl.when(pl.program_id(1) == 0)
    def _():
        for rows in chunks:
            x = x_ref[rows, :]
            h_ref[rows, :] = _rms(x, g_ref[...]).astype(BF16)
            o_ref[rows, :] = x

    for rows in chunks:
        h = h_ref[rows, :]
        a = jnp.dot(h, w1_ref[...], preferred_element_type=F32)
        b = jnp.dot(h, w3_ref[...], preferred_element_type=F32)
        z = (0.5 * a * jax.nn.sigmoid(a) * b).astype(BF16)
        o_ref[rows, :] += jnp.dot(z, w2_ref[...], preferred_element_type=F32)


def _ffn(x, g, w1, w3, w2):
    t, d = x.shape
    f = w1.shape[1]
    tm = _tile(t, 1024)
    tf = _tile(f, 512)
    return pl.pallas_call(
        functools.partial(_ffn_body, row_chunk=_tile(tm, 512)),
        out_shape=jax.ShapeDtypeStruct((t, d), F32),
        grid=(t // tm, f // tf),
        in_specs=[
            pl.BlockSpec((tm, d), lambda i, j: (i, 0), pipeline_mode=pl.Buffered(1)),
            pl.BlockSpec((1, d), lambda i, j: (0, 0)),
            pl.BlockSpec((d, tf), lambda i, j: (0, j)),
            pl.BlockSpec((d, tf), lambda i, j: (0, j)),
            pl.BlockSpec((tf, d), lambda i, j: (j, 0)),
        ],
        out_specs=pl.BlockSpec((tm, d), lambda i, j: (i, 0)),
        scratch_shapes=[pltpu.VMEM((tm, d), BF16)],
        compiler_params=_params(("parallel", "arbitrary")),
        name="ffn",
    )(x, g, w1, w3, w2)


def _inproj_body(x_ref, g_ref, w_ref, gq_ref, gk_ref, cos_ref, sin_ref,
                 qt_ref, k_ref, vt_ref, u_ref, y_ref):
    h = _rms(x_ref[0], g_ref[...]).astype(BF16)
    proj = jnp.dot(h, w_ref[...], preferred_element_type=F32)
    cos = cos_ref[...]
    sin = sin_ref[...]

    def norm_rope(z, g):
        z = _rms(z, g)
        return z * cos + pltpu.roll(z, HEAD_DIM // 2, 1) * sin

    for hd in range(N_Q_HEADS):
        q = norm_rope(proj[:, hd * HEAD_DIM:(hd + 1) * HEAD_DIM], gq_ref[...]) * Q_SCALE
        qt_ref[0, hd] = q.T.astype(BF16)
    for kv in range(N_KV_HEADS):
        c0 = ATTN_W + kv * HEAD_DIM
        k_ref[0, kv] = norm_rope(proj[:, c0:c0 + HEAD_DIM], gk_ref[...]).astype(BF16)
        c0 = ATTN_W + KV_W + kv * HEAD_DIM
        vt_ref[0, kv] = proj[:, c0:c0 + HEAD_DIM].T.astype(BF16)
    c0 = ATTN_W + 2 * KV_W
    u_ref[0] = proj[:, c0:c0 + LRU_W]
    y_ref[0] = proj[:, c0 + LRU_W:c0 + 2 * LRU_W]


def _inproj(x, g, w, gq, gk, cos, sin):
    b, s, d = x.shape
    tm = _tile(s, 512)
    n = s // tm
    return pl.pallas_call(
        _inproj_body,
        out_shape=(
            jax.ShapeDtypeStruct((b, N_Q_HEADS, HEAD_DIM, s), BF16),
            jax.ShapeDtypeStruct((b, N_KV_HEADS, s, HEAD_DIM), BF16),
            jax.ShapeDtypeStruct((b, N_KV_HEADS, HEAD_DIM, s), BF16),
            jax.ShapeDtypeStruct((b, s, LRU_W), F32),
            jax.ShapeDtypeStruct((b, s, LRU_W), F32),
        ),
        grid=(b, n),
        in_specs=[
            pl.BlockSpec((1, tm, d), lambda bi, i: (bi, i, 0)),
            pl.BlockSpec((1, d), lambda bi, i: (0, 0)),
            _resident(w.shape),
            pl.BlockSpec((1, HEAD_DIM), lambda bi, i: (0, 0)),
            pl.BlockSpec((1, HEAD_DIM), lambda bi, i: (0, 0)),
            pl.BlockSpec((tm, HEAD_DIM), lambda bi, i: (i, 0)),
            pl.BlockSpec((tm, HEAD_DIM), lambda bi, i: (i, 0)),
        ],
        out_specs=(
            pl.BlockSpec((1, N_Q_HEADS, HEAD_DIM, tm), lambda bi, i: (bi, 0, 0, i)),
            pl.BlockSpec((1, N_KV_HEADS, tm, HEAD_DIM), lambda bi, i: (bi, 0, i, 0)),
            pl.BlockSpec((1, N_KV_HEADS, HEAD_DIM, tm), lambda bi, i: (bi, 0, 0, i)),
            pl.BlockSpec((1, tm, LRU_W), lambda bi, i: (bi, i, 0)),
            pl.BlockSpec((1, tm, LRU_W), lambda bi, i: (bi, i, 0)),
        ),
        compiler_params=_params(("parallel", "parallel")),
        name="inproj",
    )(x, g, w, gq, gk, cos, sin)


def _attn_body(qt_ref, k_ref, vt_ref, o_ref, m_ref, l_ref, acc_ref, *, tk):
    s_len = k_ref.shape[2]
    m_ref[...] = jnp.full(m_ref.shape, NEG_BIG, F32)
    l_ref[...] = jnp.zeros(l_ref.shape, F32)
    acc_ref[...] = jnp.zeros(acc_ref.shape, F32)

    def kv_step(j, carry):
        off = pl.multiple_of(j * tk, tk)
        kb = k_ref[0, 0, pl.ds(off, tk), :]
        vb = vt_ref[0, 0, :, pl.ds(off, tk)]
        for g in range(GROUP):
            s = jnp.dot(kb, qt_ref[0, g], preferred_element_type=F32)
            m_old = m_ref[g]
            m_new = jnp.maximum(m_old, jnp.max(s, axis=0, keepdims=True))
            alpha = jnp.exp2(m_old - m_new)
            p = jnp.exp2(s - m_new)
            l_ref[g] = alpha * l_ref[g] + jnp.sum(p, axis=0, keepdims=True)
            acc_ref[g] = alpha * acc_ref[g] + jnp.dot(vb, p.astype(BF16), preferred_element_type=F32)
            m_ref[g] = m_new
        return carry

    lax.fori_loop(0, s_len // tk, kv_step, 0)
    for g in range(GROUP):
        o = acc_ref[g] / l_ref[g]
        o_ref[0, :, g * HEAD_DIM:(g + 1) * HEAD_DIM] = o.T.astype(o_ref.dtype)


def _attn(qt, k, vt):
    b, _, _, s = qt.shape
    tq = _tile(s, 512)
    tk = _tile(s, 512)
    return pl.pallas_call(
        functools.partial(_attn_body, tk=tk),
        out_shape=jax.ShapeDtypeStruct((b, s, ATTN_W), BF16),
        grid=(b, N_KV_HEADS, s // tq),
        in_specs=[
            pl.BlockSpec((1, GROUP, HEAD_DIM, tq), lambda bi, kv, i: (bi, kv, 0, i)),
            pl.BlockSpec((1, 1, s, HEAD_DIM), lambda bi, kv, i: (bi, kv, 0, 0)),
            pl.BlockSpec((1, 1, HEAD_DIM, s), lambda bi, kv, i: (bi, kv, 0, 0)),
        ],
        out_specs=pl.BlockSpec((1, tq, GROUP * HEAD_DIM), lambda bi, kv, i: (bi, i, kv)),
        scratch_shapes=[
            pltpu.VMEM((GROUP, 1, tq), F32),
            pltpu.VMEM((GROUP, 1, tq), F32),
            pltpu.VMEM((GROUP, HEAD_DIM, tq), F32),
        ],
        compiler_params=_params(("parallel", "parallel", "arbitrary")),
        name="attn",
    )(qt, k, vt)


def _expm1_nonpos(x):
    e = jnp.exp(x)
    return jnp.where(e == 1.0, x, (e - 1.0) * x / jnp.log(e))


def _gelu_tanh(y):
    return 0.5 * y * (1.0 + jnp.tanh(math.sqrt(2.0 / math.pi) * (y + 0.044715 * (y * y * y))))


def _lru_body(*refs, reverse, combine):
    if combine:
        (u_ref, up_ref, un_ref, cw_ref, cb_ref, wg_ref, bg_ref, nsp_ref, hf_ref, y_ref,
         o_ref, ext_ref, a_ref, b_ref, carry_ref) = refs
    else:
        (u_ref, up_ref, un_ref, cw_ref, cb_ref, wg_ref, bg_ref, nsp_ref,
         o_ref, ext_ref, a_ref, b_ref, carry_ref) = refs
    ts = u_ref.shape[1]
    i = pl.program_id(1)
    n = pl.num_programs(1)
    ti = n - 1 - i if reverse else i

    @pl.when(i == 0)
    def _():
        carry_ref[...] = jnp.zeros(carry_ref.shape, F32)

    ext_ref[0:SUBLANES, :] = jnp.where(ti > 0, up_ref[0], 0.0)
    ext_ref[SUBLANES:SUBLANES + ts, :] = u_ref[0]
    ext_ref[SUBLANES + ts:2 * SUBLANES + ts, :] = jnp.where(ti < n - 1, un_ref[0], 0.0)
    uc = cb_ref[...]
    for j in range(CONV_W):
        off = SUBLANES - CONV_LEFT + j
        uc = uc + ext_ref[off:off + ts, :] * cw_ref[j:j + 1, :]

    ucb = uc.astype(BF16)
    for hb in range(LRU_BLOCKS):
        cols = slice(hb * LRU_BW, (hb + 1) * LRU_BW)
        gates = jnp.dot(ucb[:, cols], wg_ref[hb], preferred_element_type=F32) + bg_ref[hb]
        r = jax.nn.sigmoid(gates[:, :LRU_BW])
        ig = jax.nn.sigmoid(gates[:, LRU_BW:])
        log_a = nsp_ref[:, cols] * r
        a_ref[:, cols] = jnp.exp(log_a)
        b_ref[:, cols] = jnp.sqrt(-_expm1_nonpos(2.0 * log_a)) * ig * uc[:, cols]

    nb = ts // SUBLANES
    row = lax.broadcasted_iota(jnp.int32, (SUBLANES, LRU_W), 0)

    def block(jb, carry):
        r0 = pl.multiple_of((nb - 1 - jb if reverse else jb) * SUBLANES, SUBLANES)
        a = a_ref[pl.ds(r0, SUBLANES), :]
        b = b_ref[pl.ds(r0, SUBLANES), :]
        for dist in (1, 2, 4):
            if reverse:
                shift, valid = SUBLANES - dist, row < SUBLANES - dist
            else:
                shift, valid = dist, row >= dist
            a_s = jnp.where(valid, pltpu.roll(a, shift, 0), 1.0)
            b_s = jnp.where(valid, pltpu.roll(b, shift, 0), 0.0)
            b = a * b_s + b
            a = a * a_s
        h = a * carry + b
        b_ref[pl.ds(r0, SUBLANES), :] = h
        return h[0:1, :] if reverse else h[SUBLANES - 1:SUBLANES, :]

    carry_ref[...] = lax.fori_loop(0, nb, block, carry_ref[...], unroll=4)

    if combine:
        o_ref[0] = ((hf_ref[0] + b_ref[...]) * _gelu_tanh(y_ref[0])).astype(o_ref.dtype)
    else:
        o_ref[0] = b_ref[...]


def _lru(u, cw, cb, wg, bg, nsp, hf=None, y=None, *, reverse):
    b, s, w = u.shape
    ts = _tile(s, 512)
    n = s // ts
    nb8 = ts // SUBLANES
    combine = hf is not None

    def tix(i):
        return n - 1 - i if reverse else i

    main = pl.BlockSpec((1, ts, w), lambda bi, i: (bi, tix(i), 0))
    in_specs = [
        main,
        pl.BlockSpec((1, SUBLANES, w), lambda bi, i: (bi, jnp.maximum(tix(i) * nb8 - 1, 0), 0)),
        pl.BlockSpec((1, SUBLANES, w), lambda bi, i: (bi, jnp.minimum((tix(i) + 1) * nb8, s // SUBLANES - 1), 0)),
        pl.BlockSpec(cw.shape, lambda bi, i: (0, 0)),
        pl.BlockSpec(cb.shape, lambda bi, i: (0, 0)),
        pl.BlockSpec(wg.shape, lambda bi, i: (0, 0, 0)),
        pl.BlockSpec(bg.shape, lambda bi, i: (0, 0, 0)),
        pl.BlockSpec(nsp.shape, lambda bi, i: (0, 0)),
    ]
    args = [u, u, u, cw, cb, wg, bg, nsp]
    if combine:
        in_specs += [main, main]
        args += [hf, y]
    return pl.pallas_call(
        functools.partial(_lru_body, reverse=reverse, combine=combine),
        out_shape=jax.ShapeDtypeStruct((b, s, w), BF16 if combine else F32),
        grid=(b, n),
        in_specs=in_specs,
        out_specs=main,
        scratch_shapes=[
            pltpu.VMEM((ts + 2 * SUBLANES, w), F32),
            pltpu.VMEM((ts, w), F32),
            pltpu.VMEM((ts, w), F32),
            pltpu.VMEM((1, w), F32),
        ],
        compiler_params=_params(("parallel", "arbitrary")),
        name="lru_bwd" if reverse else "lru_fwd",
    )(*args)


def _outproj_body(x_ref, a_ref, r_ref, wa_ref, wr_ref, o_ref):
    o_ref[...] = (x_ref[...]
                  + jnp.dot(a_ref[...], wa_ref[...], preferred_element_type=F32)
                  + jnp.dot(r_ref[...], wr_ref[...], preferred_element_type=F32))


def _outproj(x, attn, lru, wa, wr):
    t, d = x.shape
    tm = _tile(t, 512)
    return pl.pallas_call(
        _outproj_body,
        out_shape=jax.ShapeDtypeStruct((t, d), F32),
        grid=(t // tm,),
        in_specs=[
            pl.BlockSpec((tm, d), lambda i: (i, 0)),
            pl.BlockSpec((tm, attn.shape[1]), lambda i: (i, 0)),
            pl.BlockSpec((tm, lru.shape[1]), lambda i: (i, 0)),
            _resident(wa.shape),
            _resident(wr.shape),
        ],
        out_specs=pl.BlockSpec((tm, d), lambda i: (i, 0)),
        compiler_params=_params(("parallel",)),
        name="outproj",
    )(x, attn, lru, wa, wr)


def _ple_body(x_ref, p_ref, g_ref, wg_ref, wp_ref, gf_ref, o_ref, *, final):
    x = x_ref[...]
    h = _rms(x, g_ref[...]).astype(BF16)
    gate = jax.nn.sigmoid(jnp.dot(h, wg_ref[...], preferred_element_type=F32))
    emb = jnp.dot(p_ref[...].astype(BF16), wp_ref[...], preferred_element_type=F32)
    x = x + gate * emb
    o_ref[...] = _rms(x, gf_ref[...]) if final else x


def _ple(x, p, g, wg, wp, gf, *, final):
    t, d = x.shape
    tm = _tile(t, 512)
    return pl.pallas_call(
        functools.partial(_ple_body, final=final),
        out_shape=jax.ShapeDtypeStruct((t, d), F32),
        grid=(t // tm,),
        in_specs=[
            pl.BlockSpec((tm, d), lambda i: (i, 0)),
            pl.BlockSpec((tm, p.shape[1]), lambda i: (i, 0)),
            pl.BlockSpec((1, d), lambda i: (0, 0)),
            _resident(wg.shape),
            _resident(wp.shape),
            pl.BlockSpec((1, d), lambda i: (0, 0)),
        ],
        out_specs=pl.BlockSpec((tm, d), lambda i: (i, 0)),
        compiler_params=_params(("parallel",)),
        name="ple",
    )(x, p, g, wg, wp, gf)


def _rope_tables(seq_len):
    t = jnp.arange(seq_len, dtype=jnp.int32)
    r = (t // GRID_W).astype(F32)
    c = (t % GRID_W).astype(F32)
    inv = ROPE_THETA ** (-jnp.arange(0, AXIS_DIM, 2, dtype=F32) / AXIS_DIM)
    ang = jnp.concatenate([r[:, None] * inv, c[:, None] * inv], axis=-1)
    cos, sin = jnp.cos(ang), jnp.sin(ang)
    return jnp.concatenate([cos, cos], axis=-1), jnp.concatenate([-sin, sin], axis=-1)


def _deinterleave_perm():
    half = jnp.concatenate([jnp.arange(0, HEAD_DIM, 2), jnp.arange(1, HEAD_DIM, 2)])
    heads = jnp.arange(N_Q_HEADS + N_KV_HEADS)[:, None] * HEAD_DIM
    return (heads + half[None, :]).reshape(-1)


def kernel(x, p, norm_ffn1, w1_ffn1, w3_ffn1, w2_ffn1, norm_mix, w_in, q_norm, k_norm, conv_w, conv_b,
           lru_wa, lru_ba, lru_wi, lru_bi, lru_lambda, w_out, norm_ffn2, w1_ffn2, w3_ffn2, w2_ffn2,
           norm_ple, w_ple_gate, w_ple_proj, norm_final):
    b, s, d = x.shape
    depth = w_in.shape[0]
    t = b * s
    cos, sin = _rope_tables(s)
    perm = _deinterleave_perm()
    half = perm[:HEAD_DIM]
    qk_w = ATTN_W + KV_W

    x = x.reshape(t, d)
    for l in range(depth):
        row = lambda v: v.reshape(1, -1)
        x = _ffn(x, row(norm_ffn1[l]), w1_ffn1[l].astype(BF16), w3_ffn1[l].astype(BF16), w2_ffn1[l].astype(BF16))

        w_in_l = jnp.concatenate([w_in[l][:, :qk_w][:, perm], w_in[l][:, qk_w:]], axis=1).astype(BF16)
        qt, k, vt, u, y = _inproj(x.reshape(b, s, d), row(norm_mix[l]), w_in_l,
                                  row(q_norm[l][half]), row(k_norm[l][half]), cos, sin)
        attn = _attn(qt, k, vt)

        nsp = -LRU_C * jax.nn.softplus(-lru_lambda[l])
        wg = jnp.concatenate([lru_wa[l], lru_wi[l]], axis=-1).astype(BF16)
        bg = jnp.concatenate([lru_ba[l], lru_bi[l]], axis=-1)[:, :, None, :]
        cb = row(conv_b[l])
        h_f = _lru(u, conv_w[l], cb, wg[0], bg[0], nsp[0:1], reverse=False)
        lru = _lru(u, conv_w[l], cb, wg[1], bg[1], nsp[1:2], h_f, y, reverse=True)

        w_out_l = w_out[l].astype(BF16)
        x = _outproj(x, attn.reshape(t, ATTN_W), lru.reshape(t, LRU_W), w_out_l[:ATTN_W], w_out_l[ATTN_W:])

        x = _ffn(x, row(norm_ffn2[l]), w1_ffn2[l].astype(BF16), w3_ffn2[l].astype(BF16), w2_ffn2[l].astype(BF16))

        x = _ple(x, p[l].reshape(t, -1), row(norm_ple[l]), w_ple_gate[l].astype(BF16),
                 w_ple_proj[l].astype(BF16), row(norm_final), final=(l == depth - 1))
    if depth == 0:
        x = _rms(x, norm_final)
    return x.reshape(b, s, d)
```

```python
import functools
import math

import jax
import jax.numpy as jnp
from jax import lax
from jax.experimental import pallas as pl
from jax.experimental.pallas import tpu as pltpu

F32 = jnp.float32
BF16 = jnp.bfloat16

EPS = 1e-6
HEAD_DIM = 128
N_Q_HEADS = 8
N_KV_HEADS = 2
GROUP = N_Q_HEADS // N_KV_HEADS
ATTN_W = N_Q_HEADS * HEAD_DIM
KV_W = N_KV_HEADS * HEAD_DIM
LRU_BLOCKS = 8
LRU_BW = 128
LRU_W = LRU_BLOCKS * LRU_BW
LRU_C = 8.0
CONV_W = 4
CONV_LEFT = 2
GRID_W = 64
ROPE_THETA = 10000.0
AXIS_DIM = HEAD_DIM // 2

V7X_VMEM_BYTES = 64 * 1024 * 1024
VMEM_LIMIT_BYTES = V7X_VMEM_BYTES - 8 * 1024 * 1024
SUBLANES = 8
Q_SCALE = math.log2(math.e) / math.sqrt(HEAD_DIM)
NEG_BIG = -1e30
SAFE_SHIFT_LOG2 = 60.0


def _tile(n, pref):
    t = pref
    while n % t:
        t //= 2
    return t


def _params(sem):
    return pltpu.CompilerParams(dimension_semantics=sem, vmem_limit_bytes=VMEM_LIMIT_BYTES)


def _rms(x, g):
    return x * lax.rsqrt(jnp.mean(x * x, axis=-1, keepdims=True) + EPS) * g


def _resident(shape):
    nd = len(shape)
    return pl.BlockSpec(shape, lambda *_: (0,) * nd, pipeline_mode=pl.Buffered(1))


def _ffn_body(x_ref, g_ref, w1_ref, w3_ref, w2_ref, o_ref, h_ref, *, row_chunk):
    tm = h_ref.shape[0]
    chunks = [pl.ds(r * row_chunk, row_chunk) for r in range(tm // row_chunk)]

    @pl.when(pl.program_id(1) == 0)
    def _():
        for rows in chunks:
            x = x_ref[rows, :]
            h_ref[rows, :] = _rms(x, g_ref[...]).astype(BF16)
            o_ref[rows, :] = x

    for rows in chunks:
        h = h_ref[rows, :]
        a = jnp.dot(h, w1_ref[...], preferred_element_type=F32)
        b = jnp.dot(h, w3_ref[...], preferred_element_type=F32)
        z = (0.5 * a * jax.nn.sigmoid(a) * b).astype(BF16)
        o_ref[rows, :] += jnp.dot(z, w2_ref[...], preferred_element_type=F32)


def _ffn(x, g, w1, w3, w2):
    t, d = x.shape
    f = w1.shape[1]
    tm = _tile(t, 1024)
    tf = _tile(f, 512)
    return pl.pallas_call(
        functools.partial(_ffn_body, row_chunk=_tile(tm, 512)),
        out_shape=jax.ShapeDtypeStruct((t, d), F32),
        grid=(t // tm, f // tf),
        in_specs=[
            pl.BlockSpec((tm, d), lambda i, j: (i, 0), pipeline_mode=pl.Buffered(1)),
            pl.BlockSpec((1, d), lambda i, j: (0, 0)),
            pl.BlockSpec((d, tf), lambda i, j: (0, j)),
            pl.BlockSpec((d, tf), lambda i, j: (0, j)),
            pl.BlockSpec((tf, d), lambda i, j: (j, 0)),
        ],
        out_specs=pl.BlockSpec((tm, d), lambda i, j: (i, 0)),
        scratch_shapes=[pltpu.VMEM((tm, d), BF16)],
        compiler_params=_params(("parallel", "arbitrary")),
        name="ffn",
    )(x, g, w1, w3, w2)


def _inproj_body(x_ref, g_ref, w_ref, gq_ref, gk_ref, cos_ref, sin_ref,
                 qt_ref, k_ref, vt_ref, u_ref, y_ref):
    h = _rms(x_ref[0], g_ref[...]).astype(BF16)
    proj = jnp.dot(h, w_ref[...], preferred_element_type=F32)
    cos = cos_ref[...]
    sin = sin_ref[...]

    def norm_rope(z, g):
        z = _rms(z, g)
        return z * cos + pltpu.roll(z, HEAD_DIM // 2, 1) * sin

    for hd in range(N_Q_HEADS):
        q = norm_rope(proj[:, hd * HEAD_DIM:(hd + 1) * HEAD_DIM], gq_ref[...]) * Q_SCALE
        qt_ref[0, hd] = q.T.astype(BF16)
    for kv in range(N_KV_HEADS):
        c0 = ATTN_W + kv * HEAD_DIM
        k_ref[0, kv] = norm_rope(proj[:, c0:c0 + HEAD_DIM], gk_ref[...]).astype(BF16)
        c0 = ATTN_W + KV_W + kv * HEAD_DIM
        vt_ref[0, kv] = proj[:, c0:c0 + HEAD_DIM].T.astype(BF16)
    c0 = ATTN_W + 2 * KV_W
    u_ref[0] = proj[:, c0:c0 + LRU_W]
    y_ref[0] = proj[:, c0 + LRU_W:c0 + 2 * LRU_W]


def _inproj(x, g, w, gq, gk, cos, sin):
    b, s, d = x.shape
    tm = _tile(s, 512)
    n = s // tm
    return pl.pallas_call(
        _inproj_body,
        out_shape=(
            jax.ShapeDtypeStruct((b, N_Q_HEADS, HEAD_DIM, s), BF16),
            jax.ShapeDtypeStruct((b, N_KV_HEADS, s, HEAD_DIM), BF16),
            jax.ShapeDtypeStruct((b, N_KV_HEADS, HEAD_DIM, s), BF16),
            jax.ShapeDtypeStruct((b, s, LRU_W), F32),
            jax.ShapeDtypeStruct((b, s, LRU_W), F32),
        ),
        grid=(b, n),
        in_specs=[
            pl.BlockSpec((1, tm, d), lambda bi, i: (bi, i, 0)),
            pl.BlockSpec((1, d), lambda bi, i: (0, 0)),
            _resident(w.shape),
            pl.BlockSpec((1, HEAD_DIM), lambda bi, i: (0, 0)),
            pl.BlockSpec((1, HEAD_DIM), lambda bi, i: (0, 0)),
            pl.BlockSpec((tm, HEAD_DIM), lambda bi, i: (i, 0)),
            pl.BlockSpec((tm, HEAD_DIM), lambda bi, i: (i, 0)),
        ],
        out_specs=(
            pl.BlockSpec((1, N_Q_HEADS, HEAD_DIM, tm), lambda bi, i: (bi, 0, 0, i)),
            pl.BlockSpec((1, N_KV_HEADS, tm, HEAD_DIM), lambda bi, i: (bi, 0, i, 0)),
            pl.BlockSpec((1, N_KV_HEADS, HEAD_DIM, tm), lambda bi, i: (bi, 0, 0, i)),
            pl.BlockSpec((1, tm, LRU_W), lambda bi, i: (bi, i, 0)),
            pl.BlockSpec((1, tm, LRU_W), lambda bi, i: (bi, i, 0)),
        ),
        compiler_params=_params(("parallel", "parallel")),
        name="inproj",
    )(x, g, w, gq, gk, cos, sin)


def _attn_body(qt_ref, k_ref, vt_ref, o_ref, kmax_ref, m_ref, l_ref, acc_ref, s0_ref, s1_ref, *, tk):
    s_len = k_ref.shape[2]
    n_kv = s_len // tk

    @pl.when(pl.program_id(2) == 0)
    def _():
        def key_chunk(j, best):
            kc = k_ref[0, 0, pl.ds(pl.multiple_of(j * tk, tk), tk), :].astype(F32)
            return jnp.maximum(best, jnp.sum(kc * kc, axis=1, keepdims=True))
        best = lax.fori_loop(0, n_kv, key_chunk, jnp.zeros((tk, 1), F32))
        kmax_ref[...] = jnp.max(best, axis=0, keepdims=True)

    for g in range(GROUP):
        qf = qt_ref[0, g].astype(F32)
        m_ref[g] = jnp.sqrt(jnp.sum(qf * qf, axis=0, keepdims=True) * kmax_ref[...])
    l_ref[...] = jnp.zeros(l_ref.shape, F32)
    acc_ref[...] = jnp.zeros(acc_ref.shape, F32)
    shift_is_safe = jnp.max(m_ref[...]) <= SAFE_SHIFT_LOG2

    def key_block(j):
        return k_ref[0, 0, pl.ds(pl.multiple_of(j * tk, tk), tk), :]

    def value_block(j):
        return vt_ref[0, 0, :, pl.ds(pl.multiple_of(j * tk, tk), tk)]

    def scores(j, s_ref):
        kb = key_block(j)
        for g in range(GROUP):
            s_ref[g] = jnp.dot(kb, qt_ref[0, g], preferred_element_type=F32)

    def accumulate(j, s_ref):
        vb = value_block(j)
        for g in range(GROUP):
            p = jnp.exp2(s_ref[g] - m_ref[g])
            l_ref[g] += jnp.sum(p, axis=0, keepdims=True)
            acc_ref[g] += jnp.dot(vb, p.astype(BF16), preferred_element_type=F32)

    def online_step(j, carry):
        kb, vb = key_block(j), value_block(j)
        for g in range(GROUP):
            s = jnp.dot(kb, qt_ref[0, g], preferred_element_type=F32)
            m_old = m_ref[g]
            m_new = jnp.maximum(m_old, jnp.max(s, axis=0, keepdims=True))
            alpha = jnp.exp2(m_old - m_new)
            p = jnp.exp2(s - m_new)
            l_ref[g] = alpha * l_ref[g] + jnp.sum(p, axis=0, keepdims=True)
            acc_ref[g] = alpha * acc_ref[g] + jnp.dot(vb, p.astype(BF16), preferred_element_type=F32)
            m_ref[g] = m_new
        return carry

    @pl.when(shift_is_safe)
    def _():
        scores(0, s0_ref)

        def pair(t, carry):
            j = 2 * t
            scores(j + 1, s1_ref)
            accumulate(j, s0_ref)
            scores(j + 2, s0_ref)
            accumulate(j + 1, s1_ref)
            return carry

        lax.fori_loop(0, n_kv // 2 - 1, pair, 0)
        scores(n_kv - 1, s1_ref)
        accumulate(n_kv - 2, s0_ref)
        accumulate(n_kv - 1, s1_ref)

    @pl.when(jnp.logical_not(shift_is_safe))
    def _():
        m_ref[...] = jnp.full(m_ref.shape, NEG_BIG, F32)
        lax.fori_loop(0, n_kv, online_step, 0)

    for g in range(GROUP):
        o = acc_ref[g] / l_ref[g]
        o_ref[0, :, g * HEAD_DIM:(g + 1) * HEAD_DIM] = o.T.astype(o_ref.dtype)


def _attn(qt, k, vt):
    b, _, _, s = qt.shape
    tq = _tile(s, 512)
    tk = _tile(s // 2, 512)
    assert (s // tk) % 2 == 0
    return pl.pallas_call(
        functools.partial(_attn_body, tk=tk),
        out_shape=jax.ShapeDtypeStruct((b, s, ATTN_W), BF16),
        grid=(b, N_KV_HEADS, s // tq),
        in_specs=[
            pl.BlockSpec((1, GROUP, HEAD_DIM, tq), lambda bi, kv, i: (bi, kv, 0, i)),
            pl.BlockSpec((1, 1, s, HEAD_DIM), lambda bi, kv, i: (bi, kv, 0, 0)),
            pl.BlockSpec((1, 1, HEAD_DIM, s), lambda bi, kv, i: (bi, kv, 0, 0)),
        ],
        out_specs=pl.BlockSpec((1, tq, GROUP * HEAD_DIM), lambda bi, kv, i: (bi, i, kv)),
        scratch_shapes=[
            pltpu.VMEM((1, 1), F32),
            pltpu.VMEM((GROUP, 1, tq), F32),
            pltpu.VMEM((GROUP, 1, tq), F32),
            pltpu.VMEM((GROUP, HEAD_DIM, tq), F32),
            pltpu.VMEM((GROUP, tk, tq), F32),
            pltpu.VMEM((GROUP, tk, tq), F32),
        ],
        compiler_params=_params(("parallel", "parallel", "arbitrary")),
        name="attn",
    )(qt, k, vt)


def _expm1_nonpos(x):
    e = jnp.exp(x)
    return jnp.where(e == 1.0, x, (e - 1.0) * x / jnp.log(e))


def _gelu_tanh(y):
    return 0.5 * y * (1.0 + jnp.tanh(math.sqrt(2.0 / math.pi) * (y + 0.044715 * (y * y * y))))


def _lru_body(*refs, reverse, combine):
    if combine:
        (u_ref, up_ref, un_ref, cw_ref, cb_ref, wg_ref, bg_ref, nsp_ref, hf_ref, y_ref,
         o_ref, ext_ref, a_ref, b_ref, carry_ref) = refs
    else:
        (u_ref, up_ref, un_ref, cw_ref, cb_ref, wg_ref, bg_ref, nsp_ref,
         o_ref, ext_ref, a_ref, b_ref, carry_ref) = refs
    ts = u_ref.shape[1]
    i = pl.program_id(1)
    n = pl.num_programs(1)
    ti = n - 1 - i if reverse else i

    @pl.when(i == 0)
    def _():
        carry_ref[...] = jnp.zeros(carry_ref.shape, F32)

    ext_ref[0:SUBLANES, :] = jnp.where(ti > 0, up_ref[0], 0.0)
    ext_ref[SUBLANES:SUBLANES + ts, :] = u_ref[0]
    ext_ref[SUBLANES + ts:2 * SUBLANES + ts, :] = jnp.where(ti < n - 1, un_ref[0], 0.0)
    uc = cb_ref[...]
    for j in range(CONV_W):
        off = SUBLANES - CONV_LEFT + j
        uc = uc + ext_ref[off:off + ts, :] * cw_ref[j:j + 1, :]

    ucb = uc.astype(BF16)
    for hb in range(LRU_BLOCKS):
        cols = slice(hb * LRU_BW, (hb + 1) * LRU_BW)
        gates = jnp.dot(ucb[:, cols], wg_ref[hb], preferred_element_type=F32) + bg_ref[hb]
        r = jax.nn.sigmoid(gates[:, :LRU_BW])
        ig = jax.nn.sigmoid(gates[:, LRU_BW:])
        log_a = nsp_ref[:, cols] * r
        a_ref[:, cols] = jnp.exp(log_a)
        b_ref[:, cols] = jnp.sqrt(-_expm1_nonpos(2.0 * log_a)) * ig * uc[:, cols]

    nb = ts // SUBLANES
    row = lax.broadcasted_iota(jnp.int32, (SUBLANES, LRU_W), 0)

    def block(jb, carry):
        r0 = pl.multiple_of((nb - 1 - jb if reverse else jb) * SUBLANES, SUBLANES)
        a = a_ref[pl.ds(r0, SUBLANES), :]
        b = b_ref[pl.ds(r0, SUBLANES), :]
        for dist in (1, 2, 4):
            if reverse:
                shift, valid = SUBLANES - dist, row < SUBLANES - dist
            else:
                shift, valid = dist, row >= dist
            a_s = jnp.where(valid, pltpu.roll(a, shift, 0), 1.0)
            b_s = jnp.where(valid, pltpu.roll(b, shift, 0), 0.0)
            b = a * b_s + b
            a = a * a_s
        h = a * carry + b
        b_ref[pl.ds(r0, SUBLANES), :] = h
        return h[0:1, :] if reverse else h[SUBLANES - 1:SUBLANES, :]

    carry_ref[...] = lax.fori_loop(0, nb, block, carry_ref[...], unroll=4)

    if combine:
        o_ref[0] = ((hf_ref[0] + b_ref[...]) * _gelu_tanh(y_ref[0])).astype(o_ref.dtype)
    else:
        o_ref[0] = b_ref[...]


def _lru(u, cw, cb, wg, bg, nsp, hf=None, y=None, *, reverse):
    b, s, w = u.shape
    ts = _tile(s, 512)
    n = s // ts
    nb8 = ts // SUBLANES
    combine = hf is not None

    def tix(i):
        return n - 1 - i if reverse else i

    main = pl.BlockSpec((1, ts, w), lambda bi, i: (bi, tix(i), 0))
    in_specs = [
        main,
        pl.BlockSpec((1, SUBLANES, w), lambda bi, i: (bi, jnp.maximum(tix(i) * nb8 - 1, 0), 0)),
        pl.BlockSpec((1, SUBLANES, w), lambda bi, i: (bi, jnp.minimum((tix(i) + 1) * nb8, s // SUBLANES - 1), 0)),
        pl.BlockSpec(cw.shape, lambda bi, i: (0, 0)),
        pl.BlockSpec(cb.shape, lambda bi, i: (0, 0)),
        pl.BlockSpec(wg.shape, lambda bi, i: (0, 0, 0)),
        pl.BlockSpec(bg.shape, lambda bi, i: (0, 0, 0)),
        pl.BlockSpec(nsp.shape, lambda bi, i: (0, 0)),
    ]
    args = [u, u, u, cw, cb, wg, bg, nsp]
    if combine:
        in_specs += [main, main]
        args += [hf, y]
    return pl.pallas_call(
        functools.partial(_lru_body, reverse=reverse, combine=combine),
        out_shape=jax.ShapeDtypeStruct((b, s, w), BF16 if combine else F32),
        grid=(b, n),
        in_specs=in_specs,
        out_specs=main,
        scratch_shapes=[
            pltpu.VMEM((ts + 2 * SUBLANES, w), F32),
            pltpu.VMEM((ts, w), F32),
            pltpu.VMEM((ts, w), F32),
            pltpu.VMEM((1, w), F32),
        ],
        compiler_params=_params(("parallel", "arbitrary")),
        name="lru_bwd" if reverse else "lru_fwd",
    )(*args)


def _outproj_body(x_ref, a_ref, r_ref, wa_ref, wr_ref, o_ref):
    o_ref[...] = (x_ref[...]
                  + jnp.dot(a_ref[...], wa_ref[...], preferred_element_type=F32)
                  + jnp.dot(r_ref[...], wr_ref[...], preferred_element_type=F32))


def _outproj(x, attn, lru, wa, wr):
    t, d = x.shape
    tm = _tile(t, 512)
    return pl.pallas_call(
        _outproj_body,
        out_shape=jax.ShapeDtypeStruct((t, d), F32),
        grid=(t // tm,),
        in_specs=[
            pl.BlockSpec((tm, d), lambda i: (i, 0)),
            pl.BlockSpec((tm, attn.shape[1]), lambda i: (i, 0)),
            pl.BlockSpec((tm, lru.shape[1]), lambda i: (i, 0)),
            _resident(wa.shape),
            _resident(wr.shape),
        ],
        out_specs=pl.BlockSpec((tm, d), lambda i: (i, 0)),
        compiler_params=_params(("parallel",)),
        name="outproj",
    )(x, attn, lru, wa, wr)


def _ple_body(x_ref, p_ref, g_ref, wg_ref, wp_ref, gf_ref, o_ref, *, final):
    x = x_ref[...]
    h = _rms(x, g_ref[...]).astype(BF16)
    gate = jax.nn.sigmoid(jnp.dot(h, wg_ref[...], preferred_element_type=F32))
    emb = jnp.dot(p_ref[...].astype(BF16), wp_ref[...], preferred_element_type=F32)
    x = x + gate * emb
    o_ref[...] = _rms(x, gf_ref[...]) if final else x


def _ple(x, p, g, wg, wp, gf, *, final):
    t, d = x.shape
    tm = _tile(t, 512)
    return pl.pallas_call(
        functools.partial(_ple_body, final=final),
        out_shape=jax.ShapeDtypeStruct((t, d), F32),
        grid=(t // tm,),
        in_specs=[
            pl.BlockSpec((tm, d), lambda i: (i, 0)),
            pl.BlockSpec((tm, p.shape[1]), lambda i: (i, 0)),
            pl.BlockSpec((1, d), lambda i: (0, 0)),
            _resident(wg.shape),
            _resident(wp.shape),
            pl.BlockSpec((1, d), lambda i: (0, 0)),
        ],
        out_specs=pl.BlockSpec((tm, d), lambda i: (i, 0)),
        compiler_params=_params(("parallel",)),
        name="ple",
    )(x, p, g, wg, wp, gf)


def _rope_tables(seq_len):
    t = jnp.arange(seq_len, dtype=jnp.int32)
    r = (t // GRID_W).astype(F32)
    c = (t % GRID_W).astype(F32)
    inv = ROPE_THETA ** (-jnp.arange(0, AXIS_DIM, 2, dtype=F32) / AXIS_DIM)
    ang = jnp.concatenate([r[:, None] * inv, c[:, None] * inv], axis=-1)
    cos, sin = jnp.cos(ang), jnp.sin(ang)
    return jnp.concatenate([cos, cos], axis=-1), jnp.concatenate([-sin, sin], axis=-1)


def _deinterleave_perm():
    half = jnp.concatenate([jnp.arange(0, HEAD_DIM, 2), jnp.arange(1, HEAD_DIM, 2)])
    heads = jnp.arange(N_Q_HEADS + N_KV_HEADS)[:, None] * HEAD_DIM
    return (heads + half[None, :]).reshape(-1)


def kernel(x, p, norm_ffn1, w1_ffn1, w3_ffn1, w2_ffn1, norm_mix, w_in, q_norm, k_norm, conv_w, conv_b,
           lru_wa, lru_ba, lru_wi, lru_bi, lru_lambda, w_out, norm_ffn2, w1_ffn2, w3_ffn2, w2_ffn2,
           norm_ple, w_ple_gate, w_ple_proj, norm_final):
    b, s, d = x.shape
    depth = w_in.shape[0]
    t = b * s
    cos, sin = _rope_tables(s)
    perm = _deinterleave_perm()
    half = perm[:HEAD_DIM]
    qk_w = ATTN_W + KV_W

    x = x.reshape(t, d)
    for l in range(depth):
        row = lambda v: v.reshape(1, -1)
        x = _ffn(x, row(norm_ffn1[l]), w1_ffn1[l].astype(BF16), w3_ffn1[l].astype(BF16), w2_ffn1[l].astype(BF16))

        w_in_l = jnp.concatenate([w_in[l][:, :qk_w][:, perm], w_in[l][:, qk_w:]], axis=1).astype(BF16)
        qt, k, vt, u, y = _inproj(x.reshape(b, s, d), row(norm_mix[l]), w_in_l,
                                  row(q_norm[l][half]), row(k_norm[l][half]), cos, sin)
        attn = _attn(qt, k, vt)

        nsp = -LRU_C * jax.nn.softplus(-lru_lambda[l])
        wg = jnp.concatenate([lru_wa[l], lru_wi[l]], axis=-1).astype(BF16)
        bg = jnp.concatenate([lru_ba[l], lru_bi[l]], axis=-1)[:, :, None, :]
        cb = row(conv_b[l])
        h_f = _lru(u, conv_w[l], cb, wg[0], bg[0], nsp[0:1], reverse=False)
        lru = _lru(u, conv_w[l], cb, wg[1], bg[1], nsp[1:2], h_f, y, reverse=True)

        w_out_l = w_out[l].astype(BF16)
        x = _outproj(x, attn.reshape(t, ATTN_W), lru.reshape(t, LRU_W), w_out_l[:ATTN_W], w_out_l[ATTN_W:])

        x = _ffn(x, row(norm_ffn2[l]), w1_ffn2[l].astype(BF16), w3_ffn2[l].astype(BF16), w2_ffn2[l].astype(BF16))

        x = _ple(x, p[l].reshape(t, -1), row(norm_ple[l]), w_ple_gate[l].astype(BF16),
                 w_ple_proj[l].astype(BF16), row(norm_final), final=(l == depth - 1))
    if depth == 0:
        x = _rms(x, norm_final)
    return x.reshape(b, s, d)
```

```python
import functools
import math

import jax
import jax.numpy as jnp
from jax import lax
from jax.experimental import pallas as pl
from jax.experimental.pallas import tpu as pltpu

F32 = jnp.float32
BF16 = jnp.bfloat16

EPS = 1e-6
HEAD_DIM = 128
N_Q_HEADS = 8
N_KV_HEADS = 2
GROUP = N_Q_HEADS // N_KV_HEADS
ATTN_W = N_Q_HEADS * HEAD_DIM
KV_W = N_KV_HEADS * HEAD_DIM
LRU_BLOCKS = 8
LRU_BW = 128
LRU_W = LRU_BLOCKS * LRU_BW
LRU_C = 8.0
CONV_W = 4
CONV_LEFT = 2
GRID_W = 64
ROPE_THETA = 10000.0
AXIS_DIM = HEAD_DIM // 2

V7X_VMEM_BYTES = 64 * 1024 * 1024
VMEM_LIMIT_BYTES = V7X_VMEM_BYTES - 8 * 1024 * 1024
SUBLANES = 8
Q_SCALE = math.log2(math.e) / math.sqrt(HEAD_DIM)
NEG_BIG = -1e30
SAFE_SHIFT_LOG2 = 60.0
FFN_PARTS = 2


def _tile(n, pref):
    t = pref
    while n % t:
        t //= 2
    return t


def _params(sem):
    return pltpu.CompilerParams(dimension_semantics=sem, vmem_limit_bytes=VMEM_LIMIT_BYTES)


def _rms(x, g):
    return x * lax.rsqrt(jnp.mean(x * x, axis=-1, keepdims=True) + EPS) * g


def _resident(shape):
    nd = len(shape)
    return pl.BlockSpec(shape, lambda *_: (0,) * nd, pipeline_mode=pl.Buffered(1))


def _ffn_part_body(*refs, first, col_chunk):
    if first:
        x_ref, g_ref, w1_ref, w3_ref, w2_ref, o_ref = refs
        y_ref = x_ref
    else:
        x_ref, y_ref, g_ref, w1_ref, w3_ref, w2_ref, o_ref = refs
    h = _rms(x_ref[...], g_ref[...]).astype(BF16)
    o_ref[...] = y_ref[...]
    width = w1_ref.shape[1]
    for c0 in range(0, width, col_chunk):
        cols = slice(c0, min(c0 + col_chunk, width))
        a = jnp.dot(h, w1_ref[:, cols], preferred_element_type=F32)
        b = jnp.dot(h, w3_ref[:, cols], preferred_element_type=F32)
        z = (0.5 * a * jax.nn.sigmoid(a) * b).astype(BF16)
        o_ref[...] += jnp.dot(z, w2_ref[cols, :], preferred_element_type=F32)


def _ffn(x, g, w1, w3, w2):
    t, d = x.shape
    width = w1.shape[1] // FFN_PARTS
    tm = _tile(t, 256)
    tile = pl.BlockSpec((tm, d), lambda i: (i, 0))
    y = x
    for part in range(FFN_PARTS):
        first = part == 0
        in_specs = [tile] if first else [tile, tile]
        in_specs += [
            pl.BlockSpec((1, d), lambda i: (0, 0)),
            pl.BlockSpec((d, width), lambda i, part=part: (0, part), pipeline_mode=pl.Buffered(1)),
            pl.BlockSpec((d, width), lambda i, part=part: (0, part), pipeline_mode=pl.Buffered(1)),
            pl.BlockSpec((width, d), lambda i, part=part: (part, 0), pipeline_mode=pl.Buffered(1)),
        ]
        args = (x,) if first else (x, y)
        y = pl.pallas_call(
            functools.partial(_ffn_part_body, first=first, col_chunk=1024),
            out_shape=jax.ShapeDtypeStruct((t, d), F32),
            grid=(t // tm,),
            in_specs=in_specs,
            out_specs=tile,
            compiler_params=_params(("parallel",)),
            name="ffn",
        )(*args, g, w1, w3, w2)
    return y


def _inproj_body(x_ref, g_ref, w_ref, gq_ref, gk_ref, cos_ref, sin_ref,
                 qt_ref, k_ref, vt_ref, u_ref, y_ref):
    h = _rms(x_ref[0], g_ref[...]).astype(BF16)
    proj = jnp.dot(h, w_ref[...], preferred_element_type=F32)
    cos = cos_ref[...]
    sin = sin_ref[...]

    def norm_rope(z, g):
        z = _rms(z, g)
        return z * cos + pltpu.roll(z, HEAD_DIM // 2, 1) * sin

    for hd in range(N_Q_HEADS):
        q = norm_rope(proj[:, hd * HEAD_DIM:(hd + 1) * HEAD_DIM], gq_ref[...]) * Q_SCALE
        qt_ref[0, hd] = q.T.astype(BF16)
    for kv in range(N_KV_HEADS):
        c0 = ATTN_W + kv * HEAD_DIM
        k_ref[0, kv] = norm_rope(proj[:, c0:c0 + HEAD_DIM], gk_ref[...]).astype(BF16)
        c0 = ATTN_W + KV_W + kv * HEAD_DIM
        vt_ref[0, kv] = proj[:, c0:c0 + HEAD_DIM].T.astype(BF16)
    c0 = ATTN_W + 2 * KV_W
    u_ref[0] = proj[:, c0:c0 + LRU_W]
    y_ref[0] = proj[:, c0 + LRU_W:c0 + 2 * LRU_W]


def _inproj(x, g, w, gq, gk, cos, sin):
    b, s, d = x.shape
    tm = _tile(s, 512)
    n = s // tm
    return pl.pallas_call(
        _inproj_body,
        out_shape=(
            jax.ShapeDtypeStruct((b, N_Q_HEADS, HEAD_DIM, s), BF16),
            jax.ShapeDtypeStruct((b, N_KV_HEADS, s, HEAD_DIM), BF16),
            jax.ShapeDtypeStruct((b, N_KV_HEADS, HEAD_DIM, s), BF16),
            jax.ShapeDtypeStruct((b, s, LRU_W), F32),
            jax.ShapeDtypeStruct((b, s, LRU_W), F32),
        ),
        grid=(b, n),
        in_specs=[
            pl.BlockSpec((1, tm, d), lambda bi, i: (bi, i, 0)),
            pl.BlockSpec((1, d), lambda bi, i: (0, 0)),
            _resident(w.shape),
            pl.BlockSpec((1, HEAD_DIM), lambda bi, i: (0, 0)),
            pl.BlockSpec((1, HEAD_DIM), lambda bi, i: (0, 0)),
            pl.BlockSpec((tm, HEAD_DIM), lambda bi, i: (i, 0)),
            pl.BlockSpec((tm, HEAD_DIM), lambda bi, i: (i, 0)),
        ],
        out_specs=(
            pl.BlockSpec((1, N_Q_HEADS, HEAD_DIM, tm), lambda bi, i: (bi, 0, 0, i)),
            pl.BlockSpec((1, N_KV_HEADS, tm, HEAD_DIM), lambda bi, i: (bi, 0, i, 0)),
            pl.BlockSpec((1, N_KV_HEADS, HEAD_DIM, tm), lambda bi, i: (bi, 0, 0, i)),
            pl.BlockSpec((1, tm, LRU_W), lambda bi, i: (bi, i, 0)),
            pl.BlockSpec((1, tm, LRU_W), lambda bi, i: (bi, i, 0)),
        ),
        compiler_params=_params(("parallel", "parallel")),
        name="inproj",
    )(x, g, w, gq, gk, cos, sin)


def _attn_body(qt_ref, k_ref, vt_ref, o_ref, kmax_ref, m_ref, l_ref, acc_ref, s0_ref, s1_ref, *, tk):
    s_len = k_ref.shape[2]
    n_kv = s_len // tk

    @pl.when(pl.program_id(2) == 0)
    def _():
        def key_chunk(j, best):
            kc = k_ref[0, 0, pl.ds(pl.multiple_of(j * tk, tk), tk), :].astype(F32)
            return jnp.maximum(best, jnp.sum(kc * kc, axis=1, keepdims=True))
        best = lax.fori_loop(0, n_kv, key_chunk, jnp.zeros((tk, 1), F32))
        kmax_ref[...] = jnp.max(best, axis=0, keepdims=True)

    for g in range(GROUP):
        qf = qt_ref[0, g].astype(F32)
        m_ref[g] = jnp.sqrt(jnp.sum(qf * qf, axis=0, keepdims=True) * kmax_ref[...])
    l_ref[...] = jnp.zeros(l_ref.shape, F32)
    acc_ref[...] = jnp.zeros(acc_ref.shape, F32)
    shift_is_safe = jnp.max(m_ref[...]) <= SAFE_SHIFT_LOG2

    def key_block(j):
        return k_ref[0, 0, pl.ds(pl.multiple_of(j * tk, tk), tk), :]

    def value_block(j):
        return vt_ref[0, 0, :, pl.ds(pl.multiple_of(j * tk, tk), tk)]

    def scores(j, s_ref):
        kb = key_block(j)
        for g in range(GROUP):
            s_ref[g] = jnp.dot(kb, qt_ref[0, g], preferred_element_type=F32)

    def accumulate(j, s_ref):
        vb = value_block(j)
        for g in range(GROUP):
            p = jnp.exp2(s_ref[g] - m_ref[g])
            l_ref[g] += jnp.sum(p, axis=0, keepdims=True)
            acc_ref[g] += jnp.dot(vb, p.astype(BF16), preferred_element_type=F32)

    def online_step(j, carry):
        kb, vb = key_block(j), value_block(j)
        for g in range(GROUP):
            s = jnp.dot(kb, qt_ref[0, g], preferred_element_type=F32)
            m_old = m_ref[g]
            m_new = jnp.maximum(m_old, jnp.max(s, axis=0, keepdims=True))
            alpha = jnp.exp2(m_old - m_new)
            p = jnp.exp2(s - m_new)
            l_ref[g] = alpha * l_ref[g] + jnp.sum(p, axis=0, keepdims=True)
            acc_ref[g] = alpha * acc_ref[g] + jnp.dot(vb, p.astype(BF16), preferred_element_type=F32)
            m_ref[g] = m_new
        return carry

    @pl.when(shift_is_safe)
    def _():
        scores(0, s0_ref)

        def pair(t, carry):
            j = 2 * t
            scores(j + 1, s1_ref)
            accumulate(j, s0_ref)
            scores(j + 2, s0_ref)
            accumulate(j + 1, s1_ref)
            return carry

        lax.fori_loop(0, n_kv // 2 - 1, pair, 0)
        scores(n_kv - 1, s1_ref)
        accumulate(n_kv - 2, s0_ref)
        accumulate(n_kv - 1, s1_ref)

    @pl.when(jnp.logical_not(shift_is_safe))
    def _():
        m_ref[...] = jnp.full(m_ref.shape, NEG_BIG, F32)
        lax.fori_loop(0, n_kv, online_step, 0)

    for g in range(GROUP):
        o = acc_ref[g] / l_ref[g]
        o_ref[0, :, g * HEAD_DIM:(g + 1) * HEAD_DIM] = o.T.astype(o_ref.dtype)


def _attn(qt, k, vt):
    b, _, _, s = qt.shape
    tq = _tile(s, 512)
    tk = _tile(s // 2, 512)
    assert (s // tk) % 2 == 0
    return pl.pallas_call(
        functools.partial(_attn_body, tk=tk),
        out_shape=jax.ShapeDtypeStruct((b, s, ATTN_W), BF16),
        grid=(b, N_KV_HEADS, s // tq),
        in_specs=[
            pl.BlockSpec((1, GROUP, HEAD_DIM, tq), lambda bi, kv, i: (bi, kv, 0, i)),
            pl.BlockSpec((1, 1, s, HEAD_DIM), lambda bi, kv, i: (bi, kv, 0, 0)),
            pl.BlockSpec((1, 1, HEAD_DIM, s), lambda bi, kv, i: (bi, kv, 0, 0)),
        ],
        out_specs=pl.BlockSpec((1, tq, GROUP * HEAD_DIM), lambda bi, kv, i: (bi, i, kv)),
        scratch_shapes=[
            pltpu.VMEM((1, 1), F32),
            pltpu.VMEM((GROUP, 1, tq), F32),
            pltpu.VMEM((GROUP, 1, tq), F32),
            pltpu.VMEM((GROUP, HEAD_DIM, tq), F32),
            pltpu.VMEM((GROUP, tk, tq), F32),
            pltpu.VMEM((GROUP, tk, tq), F32),
        ],
        compiler_params=_params(("parallel", "parallel", "arbitrary")),
        name="attn",
    )(qt, k, vt)


def _expm1_nonpos(x):
    e = jnp.exp(x)
    return jnp.where(e == 1.0, x, (e - 1.0) * x / jnp.log(e))


def _gelu_tanh(y):
    return 0.5 * y * (1.0 + jnp.tanh(math.sqrt(2.0 / math.pi) * (y + 0.044715 * (y * y * y))))


def _lru_body(*refs, reverse, combine):
    if combine:
        (u_ref, up_ref, un_ref, cw_ref, cb_ref, wg_ref, bg_ref, nsp_ref, hf_ref, y_ref,
         o_ref, ext_ref, a_ref, b_ref, carry_ref) = refs
    else:
        (u_ref, up_ref, un_ref, cw_ref, cb_ref, wg_ref, bg_ref, nsp_ref,
         o_ref, ext_ref, a_ref, b_ref, carry_ref) = refs
    ts = u_ref.shape[1]
    i = pl.program_id(1)
    n = pl.num_programs(1)
    ti = n - 1 - i if reverse else i

    @pl.when(i == 0)
    def _():
        carry_ref[...] = jnp.zeros(carry_ref.shape, F32)

    ext_ref[0:SUBLANES, :] = jnp.where(ti > 0, up_ref[0], 0.0)
    ext_ref[SUBLANES:SUBLANES + ts, :] = u_ref[0]
    ext_ref[SUBLANES + ts:2 * SUBLANES + ts, :] = jnp.where(ti < n - 1, un_ref[0], 0.0)
    uc = cb_ref[...]
    for j in range(CONV_W):
        off = SUBLANES - CONV_LEFT + j
        uc = uc + ext_ref[off:off + ts, :] * cw_ref[j:j + 1, :]

    ucb = uc.astype(BF16)
    for hb in range(LRU_BLOCKS):
        cols = slice(hb * LRU_BW, (hb + 1) * LRU_BW)
        gates = jnp.dot(ucb[:, cols], wg_ref[hb], preferred_element_type=F32) + bg_ref[hb]
        r = jax.nn.sigmoid(gates[:, :LRU_BW])
        ig = jax.nn.sigmoid(gates[:, LRU_BW:])
        log_a = nsp_ref[:, cols] * r
        a_ref[:, cols] = jnp.exp(log_a)
        b_ref[:, cols] = jnp.sqrt(-_expm1_nonpos(2.0 * log_a)) * ig * uc[:, cols]

    nb = ts // SUBLANES
    row = lax.broadcasted_iota(jnp.int32, (SUBLANES, LRU_W), 0)

    def block(jb, carry):
        r0 = pl.multiple_of((nb - 1 - jb if reverse else jb) * SUBLANES, SUBLANES)
        a = a_ref[pl.ds(r0, SUBLANES), :]
        b = b_ref[pl.ds(r0, SUBLANES), :]
        for dist in (1, 2, 4):
            if reverse:
                shift, valid = SUBLANES - dist, row < SUBLANES - dist
            else:
                shift, valid = dist, row >= dist
            a_s = jnp.where(valid, pltpu.roll(a, shift, 0), 1.0)
            b_s = jnp.where(valid, pltpu.roll(b, shift, 0), 0.0)
            b = a * b_s + b
            a = a * a_s
        h = a * carry + b
        b_ref[pl.ds(r0, SUBLANES), :] = h
        return h[0:1, :] if reverse else h[SUBLANES - 1:SUBLANES, :]

    carry_ref[...] = lax.fori_loop(0, nb, block, carry_ref[...], unroll=4)

    if combine:
        o_ref[0] = ((hf_ref[0] + b_ref[...]) * _gelu_tanh(y_ref[0])).astype(o_ref.dtype)
    else:
        o_ref[0] = b_ref[...]


def _lru(u, cw, cb, wg, bg, nsp, hf=None, y=None, *, reverse):
    b, s, w = u.shape
    ts = _tile(s, 512)
    n = s // ts
    nb8 = ts // SUBLANES
    combine = hf is not None

    def tix(i):
        return n - 1 - i if reverse else i

    main = pl.BlockSpec((1, ts, w), lambda bi, i: (bi, tix(i), 0))
    in_specs = [
        main,
        pl.BlockSpec((1, SUBLANES, w), lambda bi, i: (bi, jnp.maximum(tix(i) * nb8 - 1, 0), 0)),
        pl.BlockSpec((1, SUBLANES, w), lambda bi, i: (bi, jnp.minimum((tix(i) + 1) * nb8, s // SUBLANES - 1), 0)),
        pl.BlockSpec(cw.shape, lambda bi, i: (0, 0)),
        pl.BlockSpec(cb.shape, lambda bi, i: (0, 0)),
        pl.BlockSpec(wg.shape, lambda bi, i: (0, 0, 0)),
        pl.BlockSpec(bg.shape, lambda bi, i: (0, 0, 0)),
        pl.BlockSpec(nsp.shape, lambda bi, i: (0, 0)),
    ]
    args = [u, u, u, cw, cb, wg, bg, nsp]
    if combine:
        in_specs += [main, main]
        args += [hf, y]
    return pl.pallas_call(
        functools.partial(_lru_body, reverse=reverse, combine=combine),
        out_shape=jax.ShapeDtypeStruct((b, s, w), BF16 if combine else F32),
        grid=(b, n),
        in_specs=in_specs,
        out_specs=main,
        scratch_shapes=[
            pltpu.VMEM((ts + 2 * SUBLANES, w), F32),
            pltpu.VMEM((ts, w), F32),
            pltpu.VMEM((ts, w), F32),
            pltpu.VMEM((1, w), F32),
        ],
        compiler_params=_params(("parallel", "arbitrary")),
        name="lru_bwd" if reverse else "lru_fwd",
    )(*args)


def _outproj_body(x_ref, a_ref, r_ref, wa_ref, wr_ref, o_ref):
    o_ref[...] = (x_ref[...]
                  + jnp.dot(a_ref[...], wa_ref[...], preferred_element_type=F32)
                  + jnp.dot(r_ref[...], wr_ref[...], preferred_element_type=F32))


def _outproj(x, attn, lru, wa, wr):
    t, d = x.shape
    tm = _tile(t, 512)
    return pl.pallas_call(
        _outproj_body,
        out_shape=jax.ShapeDtypeStruct((t, d), F32),
        grid=(t // tm,),
        in_specs=[
            pl.BlockSpec((tm, d), lambda i: (i, 0)),
            pl.BlockSpec((tm, attn.shape[1]), lambda i: (i, 0)),
            pl.BlockSpec((tm, lru.shape[1]), lambda i: (i, 0)),
            _resident(wa.shape),
            _resident(wr.shape),
        ],
        out_specs=pl.BlockSpec((tm, d), lambda i: (i, 0)),
        compiler_params=_params(("parallel",)),
        name="outproj",
    )(x, attn, lru, wa, wr)


def _ple_body(x_ref, p_ref, g_ref, wg_ref, wp_ref, gf_ref, o_ref, *, final):
    x = x_ref[...]
    h = _rms(x, g_ref[...]).astype(BF16)
    gate = jax.nn.sigmoid(jnp.dot(h, wg_ref[...], preferred_element_type=F32))
    emb = jnp.dot(p_ref[...].astype(BF16), wp_ref[...], preferred_element_type=F32)
    x = x + gate * emb
    o_ref[...] = _rms(x, gf_ref[...]) if final else x


def _ple(x, p, g, wg, wp, gf, *, final):
    t, d = x.shape
    tm = _tile(t, 512)
    return pl.pallas_call(
        functools.partial(_ple_body, final=final),
        out_shape=jax.ShapeDtypeStruct((t, d), F32),
        grid=(t // tm,),
        in_specs=[
            pl.BlockSpec((tm, d), lambda i: (i, 0)),
            pl.BlockSpec((tm, p.shape[1]), lambda i: (i, 0)),
            pl.BlockSpec((1, d), lambda i: (0, 0)),
            _resident(wg.shape),
            _resident(wp.shape),
            pl.BlockSpec((1, d), lambda i: (0, 0)),
        ],
        out_specs=pl.BlockSpec((tm, d), lambda i: (i, 0)),
        compiler_params=_params(("parallel",)),
        name="ple",
    )(x, p, g, wg, wp, gf)


def _rope_tables(seq_len):
    t = jnp.arange(seq_len, dtype=jnp.int32)
    r = (t // GRID_W).astype(F32)
    c = (t % GRID_W).astype(F32)
    inv = ROPE_THETA ** (-jnp.arange(0, AXIS_DIM, 2, dtype=F32) / AXIS_DIM)
    ang = jnp.concatenate([r[:, None] * inv, c[:, None] * inv], axis=-1)
    cos, sin = jnp.cos(ang), jnp.sin(ang)
    return jnp.concatenate([cos, cos], axis=-1), jnp.concatenate([-sin, sin], axis=-1)


def _deinterleave_perm():
    half = jnp.concatenate([jnp.arange(0, HEAD_DIM, 2), jnp.arange(1, HEAD_DIM, 2)])
    heads = jnp.arange(N_Q_HEADS + N_KV_HEADS)[:, None] * HEAD_DIM
    return (heads + half[None, :]).reshape(-1)


def kernel(x, p, norm_ffn1, w1_ffn1, w3_ffn1, w2_ffn1, norm_mix, w_in, q_norm, k_norm, conv_w, conv_b,
           lru_wa, lru_ba, lru_wi, lru_bi, lru_lambda, w_out, norm_ffn2, w1_ffn2, w3_ffn2, w2_ffn2,
           norm_ple, w_ple_gate, w_ple_proj, norm_final):
    b, s, d = x.shape
    depth = w_in.shape[0]
    t = b * s
    cos, sin = _rope_tables(s)
    perm = _deinterleave_perm()
    half = perm[:HEAD_DIM]
    qk_w = ATTN_W + KV_W

    x = x.reshape(t, d)
    for l in range(depth):
        row = lambda v: v.reshape(1, -1)
        x = _ffn(x, row(norm_ffn1[l]), w1_ffn1[l].astype(BF16), w3_ffn1[l].astype(BF16), w2_ffn1[l].astype(BF16))

        w_in_l = jnp.concatenate([w_in[l][:, :qk_w][:, perm], w_in[l][:, qk_w:]], axis=1).astype(BF16)
        qt, k, vt, u, y = _inproj(x.reshape(b, s, d), row(norm_mix[l]), w_in_l,
                                  row(q_norm[l][half]), row(k_norm[l][half]), cos, sin)
        attn = _attn(qt, k, vt)

        nsp = -LRU_C * jax.nn.softplus(-lru_lambda[l])
        wg = jnp.concatenate([lru_wa[l], lru_wi[l]], axis=-1).astype(BF16)
        bg = jnp.concatenate([lru_ba[l], lru_bi[l]], axis=-1)[:, :, None, :]
        cb = row(conv_b[l])
        h_f = _lru(u, conv_w[l], cb, wg[0], bg[0], nsp[0:1], reverse=False)
        lru = _lru(u, conv_w[l], cb, wg[1], bg[1], nsp[1:2], h_f, y, reverse=True)

        w_out_l = w_out[l].astype(BF16)
        x = _outproj(x, attn.reshape(t, ATTN_W), lru.reshape(t, LRU_W), w_out_l[:ATTN_W], w_out_l[ATTN_W:])

        x = _ffn(x, row(norm_ffn2[l]), w1_ffn2[l].astype(BF16), w3_ffn2[l].astype(BF16), w2_ffn2[l].astype(BF16))

        x = _ple(x, p[l].reshape(t, -1), row(norm_ple[l]), w_ple_gate[l].astype(BF16),
                 w_ple_proj[l].astype(BF16), row(norm_final), final=(l == depth - 1))
    if depth == 0:
        x = _rms(x, norm_final)
    return x.reshape(b, s, d)
```

```python
import functools
import math

import jax
import jax.numpy as jnp
from jax import lax
from jax.experimental import pallas as pl
from jax.experimental.pallas import tpu as pltpu

F32 = jnp.float32
BF16 = jnp.bfloat16

EPS = 1e-6
HEAD_DIM = 128
N_Q_HEADS = 8
N_KV_HEADS = 2
GROUP = N_Q_HEADS // N_KV_HEADS
ATTN_W = N_Q_HEADS * HEAD_DIM
KV_W = N_KV_HEADS * HEAD_DIM
LRU_BLOCKS = 8
LRU_BW = 128
LRU_W = LRU_BLOCKS * LRU_BW
LRU_C = 8.0
CONV_W = 4
CONV_LEFT = 2
GRID_W = 64
ROPE_THETA = 10000.0
AXIS_DIM = HEAD_DIM // 2

V7X_VMEM_BYTES = 64 * 1024 * 1024
VMEM_LIMIT_BYTES = V7X_VMEM_BYTES - 8 * 1024 * 1024
SUBLANES = 8
Q_SCALE = math.log2(math.e) / math.sqrt(HEAD_DIM)
NEG_BIG = -1e30
SAFE_SHIFT_LOG2 = 60.0
FFN_PARTS = 2
KV_UNROLL = 4


def _tile(n, pref):
    t = pref
    while n % t:
        t //= 2
    return t


def _params(sem):
    return pltpu.CompilerParams(dimension_semantics=sem, vmem_limit_bytes=VMEM_LIMIT_BYTES)


def _rms(x, g):
    return x * lax.rsqrt(jnp.mean(x * x, axis=-1, keepdims=True) + EPS) * g


def _resident(shape):
    nd = len(shape)
    return pl.BlockSpec(shape, lambda *_: (0,) * nd, pipeline_mode=pl.Buffered(1))


def _ffn_part_body(*refs, first, col_chunk):
    if first:
        x_ref, g_ref, w1_ref, w3_ref, w2_ref, o_ref = refs
        y_ref = x_ref
    else:
        x_ref, y_ref, g_ref, w1_ref, w3_ref, w2_ref, o_ref = refs
    h = _rms(x_ref[...], g_ref[...]).astype(BF16)
    o_ref[...] = y_ref[...]
    width = w1_ref.shape[1]
    for c0 in range(0, width, col_chunk):
        cols = slice(c0, min(c0 + col_chunk, width))
        a = jnp.dot(h, w1_ref[:, cols], preferred_element_type=F32)
        b = jnp.dot(h, w3_ref[:, cols], preferred_element_type=F32)
        z = (0.5 * a * jax.nn.sigmoid(a) * b).astype(BF16)
        o_ref[...] += jnp.dot(z, w2_ref[cols, :], preferred_element_type=F32)


def _ffn(x, g, w1, w3, w2):
    t, d = x.shape
    width = w1.shape[1] // FFN_PARTS
    tm = _tile(t, 256)
    tile = pl.BlockSpec((tm, d), lambda i: (i, 0))
    y = x
    for part in range(FFN_PARTS):
        first = part == 0
        in_specs = [tile] if first else [tile, tile]
        in_specs += [
            pl.BlockSpec((1, d), lambda i: (0, 0)),
            pl.BlockSpec((d, width), lambda i, part=part: (0, part), pipeline_mode=pl.Buffered(1)),
            pl.BlockSpec((d, width), lambda i, part=part: (0, part), pipeline_mode=pl.Buffered(1)),
            pl.BlockSpec((width, d), lambda i, part=part: (part, 0), pipeline_mode=pl.Buffered(1)),
        ]
        args = (x,) if first else (x, y)
        y = pl.pallas_call(
            functools.partial(_ffn_part_body, first=first, col_chunk=1024),
            out_shape=jax.ShapeDtypeStruct((t, d), F32),
            grid=(t // tm,),
            in_specs=in_specs,
            out_specs=tile,
            compiler_params=_params(("parallel",)),
            name="ffn",
        )(*args, g, w1, w3, w2)
    return y


def _inproj_body(x_ref, g_ref, w_ref, gq_ref, gk_ref, cos_ref, sin_ref,
                 qt_ref, k_ref, vt_ref, u_ref, y_ref):
    h = _rms(x_ref[0], g_ref[...]).astype(BF16)
    proj = jnp.dot(h, w_ref[...], preferred_element_type=F32)
    cos = cos_ref[...]
    sin = sin_ref[...]

    def norm_rope(z, g):
        z = _rms(z, g)
        return z * cos + pltpu.roll(z, HEAD_DIM // 2, 1) * sin

    for hd in range(N_Q_HEADS):
        q = norm_rope(proj[:, hd * HEAD_DIM:(hd + 1) * HEAD_DIM], gq_ref[...]) * Q_SCALE
        qt_ref[0, hd] = q.T.astype(BF16)
    for kv in range(N_KV_HEADS):
        c0 = ATTN_W + kv * HEAD_DIM
        k_ref[0, kv] = norm_rope(proj[:, c0:c0 + HEAD_DIM], gk_ref[...]).astype(BF16)
        c0 = ATTN_W + KV_W + kv * HEAD_DIM
        vt_ref[0, kv] = proj[:, c0:c0 + HEAD_DIM].T.astype(BF16)
    c0 = ATTN_W + 2 * KV_W
    u_ref[0] = proj[:, c0:c0 + LRU_W]
    y_ref[0] = proj[:, c0 + LRU_W:c0 + 2 * LRU_W]


def _inproj(x, g, w, gq, gk, cos, sin):
    b, s, d = x.shape
    tm = _tile(s, 512)
    n = s // tm
    return pl.pallas_call(
        _inproj_body,
        out_shape=(
            jax.ShapeDtypeStruct((b, N_Q_HEADS, HEAD_DIM, s), BF16),
            jax.ShapeDtypeStruct((b, N_KV_HEADS, s, HEAD_DIM), BF16),
            jax.ShapeDtypeStruct((b, N_KV_HEADS, HEAD_DIM, s), BF16),
            jax.ShapeDtypeStruct((b, s, LRU_W), F32),
            jax.ShapeDtypeStruct((b, s, LRU_W), F32),
        ),
        grid=(b, n),
        in_specs=[
            pl.BlockSpec((1, tm, d), lambda bi, i: (bi, i, 0)),
            pl.BlockSpec((1, d), lambda bi, i: (0, 0)),
            _resident(w.shape),
            pl.BlockSpec((1, HEAD_DIM), lambda bi, i: (0, 0)),
            pl.BlockSpec((1, HEAD_DIM), lambda bi, i: (0, 0)),
            pl.BlockSpec((tm, HEAD_DIM), lambda bi, i: (i, 0)),
            pl.BlockSpec((tm, HEAD_DIM), lambda bi, i: (i, 0)),
        ],
        out_specs=(
            pl.BlockSpec((1, N_Q_HEADS, HEAD_DIM, tm), lambda bi, i: (bi, 0, 0, i)),
            pl.BlockSpec((1, N_KV_HEADS, tm, HEAD_DIM), lambda bi, i: (bi, 0, i, 0)),
            pl.BlockSpec((1, N_KV_HEADS, HEAD_DIM, tm), lambda bi, i: (bi, 0, 0, i)),
            pl.BlockSpec((1, tm, LRU_W), lambda bi, i: (bi, i, 0)),
            pl.BlockSpec((1, tm, LRU_W), lambda bi, i: (bi, i, 0)),
        ),
        compiler_params=_params(("parallel", "parallel")),
        name="inproj",
    )(x, g, w, gq, gk, cos, sin)


def _attn_body(qt_ref, k_ref, vt_ref, o_ref, kmax_ref, m_ref, l_ref, acc_ref, s0_ref, s1_ref, *, tk):
    s_len = k_ref.shape[2]
    n_kv = s_len // tk

    @pl.when(pl.program_id(2) == 0)
    def _():
        def key_chunk(j, best):
            kc = k_ref[0, 0, pl.ds(pl.multiple_of(j * tk, tk), tk), :].astype(F32)
            return jnp.maximum(best, jnp.sum(kc * kc, axis=1, keepdims=True))
        best = lax.fori_loop(0, n_kv, key_chunk, jnp.zeros((tk, 1), F32))
        kmax_ref[...] = jnp.max(best, axis=0, keepdims=True)

    for g in range(GROUP):
        qf = qt_ref[0, g].astype(F32)
        m_ref[g] = jnp.sqrt(jnp.sum(qf * qf, axis=0, keepdims=True) * kmax_ref[...])
    l_ref[...] = jnp.zeros(l_ref.shape, F32)
    acc_ref[...] = jnp.zeros(acc_ref.shape, F32)
    shift_is_safe = jnp.max(m_ref[...]) <= SAFE_SHIFT_LOG2

    def key_block(j):
        return k_ref[0, 0, pl.ds(pl.multiple_of(j * tk, tk), tk), :]

    def value_block(j):
        return vt_ref[0, 0, :, pl.ds(pl.multiple_of(j * tk, tk), tk)]

    def scores(j, s_ref):
        kb = key_block(j)
        for g in range(GROUP):
            s_ref[g] = jnp.dot(kb, qt_ref[0, g], preferred_element_type=F32)

    def accumulate(j, s_ref):
        vb = value_block(j)
        for g in range(GROUP):
            p = jnp.exp2(s_ref[g] - m_ref[g])
            l_ref[g] += jnp.sum(p, axis=0, keepdims=True)
            acc_ref[g] += jnp.dot(vb, p.astype(BF16), preferred_element_type=F32)

    def online_step(j, carry):
        kb, vb = key_block(j), value_block(j)
        for g in range(GROUP):
            s = jnp.dot(kb, qt_ref[0, g], preferred_element_type=F32)
            m_old = m_ref[g]
            m_new = jnp.maximum(m_old, jnp.max(s, axis=0, keepdims=True))
            alpha = jnp.exp2(m_old - m_new)
            p = jnp.exp2(s - m_new)
            l_ref[g] = alpha * l_ref[g] + jnp.sum(p, axis=0, keepdims=True)
            acc_ref[g] = alpha * acc_ref[g] + jnp.dot(vb, p.astype(BF16), preferred_element_type=F32)
            m_ref[g] = m_new
        return carry

    @pl.when(shift_is_safe)
    def _():
        bufs = (s0_ref, s1_ref)
        scores(0, s0_ref)

        def group(t, carry):
            for u in range(KV_UNROLL):
                j = KV_UNROLL * t + u
                scores(j + 1, bufs[(u + 1) % 2])
                accumulate(j, bufs[u % 2])
            return carry

        lax.fori_loop(0, n_kv // KV_UNROLL - 1, group, 0)
        for u in range(KV_UNROLL):
            j = n_kv - KV_UNROLL + u
            if u + 1 < KV_UNROLL:
                scores(j + 1, bufs[(u + 1) % 2])
            accumulate(j, bufs[u % 2])

    @pl.when(jnp.logical_not(shift_is_safe))
    def _():
        m_ref[...] = jnp.full(m_ref.shape, NEG_BIG, F32)
        lax.fori_loop(0, n_kv, online_step, 0)

    for g in range(GROUP):
        o = acc_ref[g] / l_ref[g]
        o_ref[0, :, g * HEAD_DIM:(g + 1) * HEAD_DIM] = o.T.astype(o_ref.dtype)


def _attn(qt, k, vt):
    b, _, _, s = qt.shape
    tq = _tile(s, 512)
    tk = _tile(s // KV_UNROLL, 512)
    assert (s // tk) % KV_UNROLL == 0
    return pl.pallas_call(
        functools.partial(_attn_body, tk=tk),
        out_shape=jax.ShapeDtypeStruct((b, s, ATTN_W), BF16),
        grid=(b, N_KV_HEADS, s // tq),
        in_specs=[
            pl.BlockSpec((1, GROUP, HEAD_DIM, tq), lambda bi, kv, i: (bi, kv, 0, i)),
            pl.BlockSpec((1, 1, s, HEAD_DIM), lambda bi, kv, i: (bi, kv, 0, 0)),
            pl.BlockSpec((1, 1, HEAD_DIM, s), lambda bi, kv, i: (bi, kv, 0, 0)),
        ],
        out_specs=pl.BlockSpec((1, tq, GROUP * HEAD_DIM), lambda bi, kv, i: (bi, i, kv)),
        scratch_shapes=[
            pltpu.VMEM((1, 1), F32),
            pltpu.VMEM((GROUP, 1, tq), F32),
            pltpu.VMEM((GROUP, 1, tq), F32),
            pltpu.VMEM((GROUP, HEAD_DIM, tq), F32),
            pltpu.VMEM((GROUP, tk, tq), F32),
            pltpu.VMEM((GROUP, tk, tq), F32),
        ],
        compiler_params=_params(("parallel", "parallel", "arbitrary")),
        name="attn",
    )(qt, k, vt)


def _expm1_nonpos(x):
    e = jnp.exp(x)
    return jnp.where(e == 1.0, x, (e - 1.0) * x / jnp.log(e))


def _gelu_tanh(y):
    return 0.5 * y * (1.0 + jnp.tanh(math.sqrt(2.0 / math.pi) * (y + 0.044715 * (y * y * y))))


def _lru_scan(uc, wg_ref, bg_ref, nsp_ref, a_ref, b_ref, carry_ref, *, reverse):
    ts = uc.shape[0]

    @pl.when(pl.program_id(1) == 0)
    def _():
        carry_ref[...] = jnp.zeros(carry_ref.shape, F32)

    ucb = uc.astype(BF16)
    for hb in range(LRU_BLOCKS):
        cols = slice(hb * LRU_BW, (hb + 1) * LRU_BW)
        gates = jnp.dot(ucb[:, cols], wg_ref[hb], preferred_element_type=F32) + bg_ref[hb]
        r = jax.nn.sigmoid(gates[:, :LRU_BW])
        ig = jax.nn.sigmoid(gates[:, LRU_BW:])
        log_a = nsp_ref[:, cols] * r
        a_ref[:, cols] = jnp.exp(log_a)
        b_ref[:, cols] = jnp.sqrt(-_expm1_nonpos(2.0 * log_a)) * ig * uc[:, cols]

    nb = ts // SUBLANES
    row = lax.broadcasted_iota(jnp.int32, (SUBLANES, LRU_W), 0)

    def block(jb, carry):
        r0 = pl.multiple_of((nb - 1 - jb if reverse else jb) * SUBLANES, SUBLANES)
        a = a_ref[pl.ds(r0, SUBLANES), :]
        b = b_ref[pl.ds(r0, SUBLANES), :]
        for dist in (1, 2, 4):
            if reverse:
                shift, valid = SUBLANES - dist, row < SUBLANES - dist
            else:
                shift, valid = dist, row >= dist
            a_s = jnp.where(valid, pltpu.roll(a, shift, 0), 1.0)
            b_s = jnp.where(valid, pltpu.roll(b, shift, 0), 0.0)
            b = a * b_s + b
            a = a * a_s
        h = a * carry + b
        b_ref[pl.ds(r0, SUBLANES), :] = h
        return h[0:1, :] if reverse else h[SUBLANES - 1:SUBLANES, :]

    carry_ref[...] = lax.fori_loop(0, nb, block, carry_ref[...], unroll=4)


def _lru_fwd_body(u_ref, up_ref, un_ref, cw_ref, cb_ref, wg_ref, bg_ref, nsp_ref,
                  h_ref, uc_ref, ext_ref, a_ref, b_ref, carry_ref):
    ts = u_ref.shape[1]
    i = pl.program_id(1)
    ext_ref[0:SUBLANES, :] = jnp.where(i > 0, up_ref[0], 0.0)
    ext_ref[SUBLANES:SUBLANES + ts, :] = u_ref[0]
    ext_ref[SUBLANES + ts:2 * SUBLANES + ts, :] = jnp.where(i < pl.num_programs(1) - 1, un_ref[0], 0.0)
    ext = ext_ref[...]
    uc = cb_ref[...]
    for j in range(CONV_W):
        shift = (CONV_LEFT - j) % ext.shape[0]
        tap = pltpu.roll(ext, shift, 0) if shift else ext
        uc = uc + tap[SUBLANES:SUBLANES + ts, :] * cw_ref[j:j + 1, :]
    uc_ref[0] = uc
    _lru_scan(uc, wg_ref, bg_ref, nsp_ref, a_ref, b_ref, carry_ref, reverse=False)
    h_ref[0] = b_ref[...]


def _lru_bwd_body(uc_ref, wg_ref, bg_ref, nsp_ref, hf_ref, y_ref, o_ref, a_ref, b_ref, carry_ref):
    _lru_scan(uc_ref[0], wg_ref, bg_ref, nsp_ref, a_ref, b_ref, carry_ref, reverse=True)
    o_ref[0] = ((hf_ref[0] + b_ref[...]) * _gelu_tanh(y_ref[0])).astype(o_ref.dtype)


def _lru(u, y, cw, cb, wg, bg, nsp):
    b, s, w = u.shape
    ts = _tile(s, 512)
    n = s // ts
    nb8 = ts // SUBLANES

    def whole(a):
        return pl.BlockSpec(a.shape, lambda bi, i: (0,) * a.ndim)

    fwd_tile = pl.BlockSpec((1, ts, w), lambda bi, i: (bi, i, 0))
    bwd_tile = pl.BlockSpec((1, ts, w), lambda bi, i: (bi, n - 1 - i, 0))
    scan_scratch = [pltpu.VMEM((ts, w), F32), pltpu.VMEM((ts, w), F32), pltpu.VMEM((1, w), F32)]
    h_f, uc = pl.pallas_call(
        _lru_fwd_body,
        out_shape=(jax.ShapeDtypeStruct((b, s, w), F32), jax.ShapeDtypeStruct((b, s, w), F32)),
        grid=(b, n),
        in_specs=[
            fwd_tile,
            pl.BlockSpec((1, SUBLANES, w), lambda bi, i: (bi, jnp.maximum(i * nb8 - 1, 0), 0)),
            pl.BlockSpec((1, SUBLANES, w), lambda bi, i: (bi, jnp.minimum((i + 1) * nb8, s // SUBLANES - 1), 0)),
            whole(cw), whole(cb), whole(wg[0]), whole(bg[0]), whole(nsp[0:1]),
        ],
        out_specs=(fwd_tile, fwd_tile),
        scratch_shapes=[pltpu.VMEM((ts + 2 * SUBLANES, w), F32)] + scan_scratch,
        compiler_params=_params(("parallel", "arbitrary")),
        name="lru_fwd",
    )(u, u, u, cw, cb, wg[0], bg[0], nsp[0:1])
    return pl.pallas_call(
        _lru_bwd_body,
        out_shape=jax.ShapeDtypeStruct((b, s, w), BF16),
        grid=(b, n),
        in_specs=[bwd_tile, whole(wg[1]), whole(bg[1]), whole(nsp[1:2]), bwd_tile, bwd_tile],
        out_specs=bwd_tile,
        scratch_shapes=scan_scratch,
        compiler_params=_params(("parallel", "arbitrary")),
        name="lru_bwd",
    )(uc, wg[1], bg[1], nsp[1:2], h_f, y)


def _outproj_body(x_ref, a_ref, r_ref, wa_ref, wr_ref, o_ref):
    o_ref[...] = (x_ref[...]
                  + jnp.dot(a_ref[...], wa_ref[...], preferred_element_type=F32)
                  + jnp.dot(r_ref[...], wr_ref[...], preferred_element_type=F32))


def _outproj(x, attn, lru, wa, wr):
    t, d = x.shape
    tm = _tile(t, 512)
    return pl.pallas_call(
        _outproj_body,
        out_shape=jax.ShapeDtypeStruct((t, d), F32),
        grid=(t // tm,),
        in_specs=[
            pl.BlockSpec((tm, d), lambda i: (i, 0)),
            pl.BlockSpec((tm, attn.shape[1]), lambda i: (i, 0)),
            pl.BlockSpec((tm, lru.shape[1]), lambda i: (i, 0)),
            _resident(wa.shape),
            _resident(wr.shape),
        ],
        out_specs=pl.BlockSpec((tm, d), lambda i: (i, 0)),
        compiler_params=_params(("parallel",)),
        name="outproj",
    )(x, attn, lru, wa, wr)


def _ple_body(x_ref, p_ref, g_ref, wg_ref, wp_ref, gf_ref, o_ref, *, final):
    x = x_ref[...]
    h = _rms(x, g_ref[...]).astype(BF16)
    gate = jax.nn.sigmoid(jnp.dot(h, wg_ref[...], preferred_element_type=F32))
    emb = jnp.dot(p_ref[...].astype(BF16), wp_ref[...], preferred_element_type=F32)
    x = x + gate * emb
    o_ref[...] = _rms(x, gf_ref[...]) if final else x


def _ple(x, p, g, wg, wp, gf, *, final):
    t, d = x.shape
    tm = _tile(t, 512)
    return pl.pallas_call(
        functools.partial(_ple_body, final=final),
        out_shape=jax.ShapeDtypeStruct((t, d), F32),
        grid=(t // tm,),
        in_specs=[
            pl.BlockSpec((tm, d), lambda i: (i, 0)),
            pl.BlockSpec((tm, p.shape[1]), lambda i: (i, 0)),
            pl.BlockSpec((1, d), lambda i: (0, 0)),
            _resident(wg.shape),
            _resident(wp.shape),
            pl.BlockSpec((1, d), lambda i: (0, 0)),
        ],
        out_specs=pl.BlockSpec((tm, d), lambda i: (i, 0)),
        compiler_params=_params(("parallel",)),
        name="ple",
    )(x, p, g, wg, wp, gf)


def _rope_tables(seq_len):
    t = jnp.arange(seq_len, dtype=jnp.int32)
    r = (t // GRID_W).astype(F32)
    c = (t % GRID_W).astype(F32)
    inv = ROPE_THETA ** (-jnp.arange(0, AXIS_DIM, 2, dtype=F32) / AXIS_DIM)
    ang = jnp.concatenate([r[:, None] * inv, c[:, None] * inv], axis=-1)
    cos, sin = jnp.cos(ang), jnp.sin(ang)
    return jnp.concatenate([cos, cos], axis=-1), jnp.concatenate([-sin, sin], axis=-1)


def _deinterleave_perm():
    half = jnp.concatenate([jnp.arange(0, HEAD_DIM, 2), jnp.arange(1, HEAD_DIM, 2)])
    heads = jnp.arange(N_Q_HEADS + N_KV_HEADS)[:, None] * HEAD_DIM
    return (heads + half[None, :]).reshape(-1)


def kernel(x, p, norm_ffn1, w1_ffn1, w3_ffn1, w2_ffn1, norm_mix, w_in, q_norm, k_norm, conv_w, conv_b,
           lru_wa, lru_ba, lru_wi, lru_bi, lru_lambda, w_out, norm_ffn2, w1_ffn2, w3_ffn2, w2_ffn2,
           norm_ple, w_ple_gate, w_ple_proj, norm_final):
    b, s, d = x.shape
    depth = w_in.shape[0]
    t = b * s
    cos, sin = _rope_tables(s)
    perm = _deinterleave_perm()
    half = perm[:HEAD_DIM]
    qk_w = ATTN_W + KV_W

    x = x.reshape(t, d)
    for l in range(depth):
        row = lambda v: v.reshape(1, -1)
        x = _ffn(x, row(norm_ffn1[l]), w1_ffn1[l].astype(BF16), w3_ffn1[l].astype(BF16), w2_ffn1[l].astype(BF16))

        w_in_l = jnp.concatenate([w_in[l][:, :qk_w][:, perm], w_in[l][:, qk_w:]], axis=1).astype(BF16)
        qt, k, vt, u, y = _inproj(x.reshape(b, s, d), row(norm_mix[l]), w_in_l,
                                  row(q_norm[l][half]), row(k_norm[l][half]), cos, sin)
        attn = _attn(qt, k, vt)

        nsp = -LRU_C * jax.nn.softplus(-lru_lambda[l])
        wg = jnp.concatenate([lru_wa[l], lru_wi[l]], axis=-1).astype(BF16)
        bg = jnp.concatenate([lru_ba[l], lru_bi[l]], axis=-1)[:, :, None, :]
        lru = _lru(u, y, conv_w[l], row(conv_b[l]), wg, bg, nsp)

        w_out_l = w_out[l].astype(BF16)
        x = _outproj(x, attn.reshape(t, ATTN_W), lru.reshape(t, LRU_W), w_out_l[:ATTN_W], w_out_l[ATTN_W:])

        x = _ffn(x, row(norm_ffn2[l]), w1_ffn2[l].astype(BF16), w3_ffn2[l].astype(BF16), w2_ffn2[l].astype(BF16))

        x = _ple(x, p[l].reshape(t, -1), row(norm_ple[l]), w_ple_gate[l].astype(BF16),
                 w_ple_proj[l].astype(BF16), row(norm_final), final=(l == depth - 1))
    if depth == 0:
        x = _rms(x, norm_final)
    return x.reshape(b, s, d)
```

```python
import functools
import math

import jax
import jax.numpy as jnp
from jax import lax
from jax.experimental import pallas as pl
from jax.experimental.pallas import tpu as pltpu

F32 = jnp.float32
BF16 = jnp.bfloat16

EPS = 1e-6
HEAD_DIM = 128
N_Q_HEADS = 8
N_KV_HEADS = 2
GROUP = N_Q_HEADS // N_KV_HEADS
ATTN_W = N_Q_HEADS * HEAD_DIM
KV_W = N_KV_HEADS * HEAD_DIM
LRU_BLOCKS = 8
LRU_BW = 128
LRU_W = LRU_BLOCKS * LRU_BW
LRU_C = 8.0
CONV_W = 4
CONV_LEFT = 2
GRID_W = 64
ROPE_THETA = 10000.0
AXIS_DIM = HEAD_DIM // 2

V7X_VMEM_BYTES = 64 * 1024 * 1024
VMEM_LIMIT_BYTES = V7X_VMEM_BYTES - 8 * 1024 * 1024
SUBLANES = 8
Q_SCALE = math.log2(math.e) / math.sqrt(HEAD_DIM)
NEG_BIG = -1e30
SAFE_SHIFT_LOG2 = 60.0
FFN_PARTS = 2
KV_UNROLL = 8


def _tile(n, pref):
    t = pref
    while n % t:
        t //= 2
    return t


def _params(sem):
    return pltpu.CompilerParams(dimension_semantics=sem, vmem_limit_bytes=VMEM_LIMIT_BYTES)


def _rms(x, g):
    return x * lax.rsqrt(jnp.mean(x * x, axis=-1, keepdims=True) + EPS) * g


def _resident(shape):
    nd = len(shape)
    return pl.BlockSpec(shape, lambda *_: (0,) * nd, pipeline_mode=pl.Buffered(1))


def _ffn_part_body(*refs, first, col_chunk):
    if first:
        x_ref, g_ref, w1_ref, w3_ref, w2_ref, o_ref = refs
        y_ref = x_ref
    else:
        x_ref, y_ref, g_ref, w1_ref, w3_ref, w2_ref, o_ref = refs
    h = _rms(x_ref[...], g_ref[...]).astype(BF16)
    o_ref[...] = y_ref[...]
    width = w1_ref.shape[1]
    for c0 in range(0, width, col_chunk):
        cols = slice(c0, min(c0 + col_chunk, width))
        a = jnp.dot(h, w1_ref[:, cols], preferred_element_type=F32)
        b = jnp.dot(h, w3_ref[:, cols], preferred_element_type=F32)
        z = (0.5 * a * jax.nn.sigmoid(a) * b).astype(BF16)
        o_ref[...] += jnp.dot(z, w2_ref[cols, :], preferred_element_type=F32)


def _ffn(x, g, w1, w3, w2):
    t, d = x.shape
    width = w1.shape[1] // FFN_PARTS
    tm = _tile(t, 256)
    tile = pl.BlockSpec((tm, d), lambda i: (i, 0))
    y = x
    for part in range(FFN_PARTS):
        first = part == 0
        in_specs = [tile] if first else [tile, tile]
        in_specs += [
            pl.BlockSpec((1, d), lambda i: (0, 0)),
            pl.BlockSpec((d, width), lambda i, part=part: (0, part), pipeline_mode=pl.Buffered(1)),
            pl.BlockSpec((d, width), lambda i, part=part: (0, part), pipeline_mode=pl.Buffered(1)),
            pl.BlockSpec((width, d), lambda i, part=part: (part, 0), pipeline_mode=pl.Buffered(1)),
        ]
        args = (x,) if first else (x, y)
        y = pl.pallas_call(
            functools.partial(_ffn_part_body, first=first, col_chunk=1024),
            out_shape=jax.ShapeDtypeStruct((t, d), F32),
            grid=(t // tm,),
            in_specs=in_specs,
            out_specs=tile,
            compiler_params=_params(("parallel",)),
            name="ffn",
        )(*args, g, w1, w3, w2)
    return y


def _inproj_body(x_ref, g_ref, w_ref, gq_ref, gk_ref, cos_ref, sin_ref,
                 qt_ref, k_ref, vt_ref, u_ref, y_ref):
    h = _rms(x_ref[0], g_ref[...]).astype(BF16)
    proj = jnp.dot(h, w_ref[...], preferred_element_type=F32)
    cos = cos_ref[...]
    sin = sin_ref[...]

    def norm_rope(z, g):
        z = _rms(z, g)
        return z * cos + pltpu.roll(z, HEAD_DIM // 2, 1) * sin

    for hd in range(N_Q_HEADS):
        q = norm_rope(proj[:, hd * HEAD_DIM:(hd + 1) * HEAD_DIM], gq_ref[...]) * Q_SCALE
        qt_ref[0, hd] = q.T.astype(BF16)
    for kv in range(N_KV_HEADS):
        c0 = ATTN_W + kv * HEAD_DIM
        k_ref[0, kv] = norm_rope(proj[:, c0:c0 + HEAD_DIM], gk_ref[...]).astype(BF16)
        c0 = ATTN_W + KV_W + kv * HEAD_DIM
        vt_ref[0, kv] = proj[:, c0:c0 + HEAD_DIM].T.astype(BF16)
    c0 = ATTN_W + 2 * KV_W
    u_ref[0] = proj[:, c0:c0 + LRU_W]
    y_ref[0] = proj[:, c0 + LRU_W:c0 + 2 * LRU_W]


def _inproj(x, g, w, gq, gk, cos, sin):
    b, s, d = x.shape
    tm = _tile(s, 512)
    n = s // tm
    return pl.pallas_call(
        _inproj_body,
        out_shape=(
            jax.ShapeDtypeStruct((b, N_Q_HEADS, HEAD_DIM, s), BF16),
            jax.ShapeDtypeStruct((b, N_KV_HEADS, s, HEAD_DIM), BF16),
            jax.ShapeDtypeStruct((b, N_KV_HEADS, HEAD_DIM, s), BF16),
            jax.ShapeDtypeStruct((b, s, LRU_W), F32),
            jax.ShapeDtypeStruct((b, s, LRU_W), F32),
        ),
        grid=(b, n),
        in_specs=[
            pl.BlockSpec((1, tm, d), lambda bi, i: (bi, i, 0)),
            pl.BlockSpec((1, d), lambda bi, i: (0, 0)),
            _resident(w.shape),
            pl.BlockSpec((1, HEAD_DIM), lambda bi, i: (0, 0)),
            pl.BlockSpec((1, HEAD_DIM), lambda bi, i: (0, 0)),
            pl.BlockSpec((tm, HEAD_DIM), lambda bi, i: (i, 0)),
            pl.BlockSpec((tm, HEAD_DIM), lambda bi, i: (i, 0)),
        ],
        out_specs=(
            pl.BlockSpec((1, N_Q_HEADS, HEAD_DIM, tm), lambda bi, i: (bi, 0, 0, i)),
            pl.BlockSpec((1, N_KV_HEADS, tm, HEAD_DIM), lambda bi, i: (bi, 0, i, 0)),
            pl.BlockSpec((1, N_KV_HEADS, HEAD_DIM, tm), lambda bi, i: (bi, 0, 0, i)),
            pl.BlockSpec((1, tm, LRU_W), lambda bi, i: (bi, i, 0)),
            pl.BlockSpec((1, tm, LRU_W), lambda bi, i: (bi, i, 0)),
        ),
        compiler_params=_params(("parallel", "parallel")),
        name="inproj",
    )(x, g, w, gq, gk, cos, sin)


def _attn_body(qt_ref, k_ref, vt_ref, o_ref, kmax_ref, m_ref, l_ref, acc_ref, s0_ref, s1_ref, *, tk):
    s_len = k_ref.shape[2]
    n_kv = s_len // tk

    @pl.when(pl.program_id(2) == 0)
    def _():
        def key_chunk(j, best):
            kc = k_ref[0, 0, pl.ds(pl.multiple_of(j * tk, tk), tk), :].astype(F32)
            return jnp.maximum(best, jnp.sum(kc * kc, axis=1, keepdims=True))
        best = lax.fori_loop(0, n_kv, key_chunk, jnp.zeros((tk, 1), F32))
        kmax_ref[...] = jnp.max(best, axis=0, keepdims=True)

    for g in range(GROUP):
        qf = qt_ref[0, g].astype(F32)
        m_ref[g] = jnp.sqrt(jnp.sum(qf * qf, axis=0, keepdims=True) * kmax_ref[...])
    l_ref[...] = jnp.zeros(l_ref.shape, F32)
    acc_ref[...] = jnp.zeros(acc_ref.shape, F32)
    shift_is_safe = jnp.max(m_ref[...]) <= SAFE_SHIFT_LOG2

    def key_block(j):
        return k_ref[0, 0, pl.ds(pl.multiple_of(j * tk, tk), tk), :]

    def value_block(j):
        return vt_ref[0, 0, :, pl.ds(pl.multiple_of(j * tk, tk), tk)]

    def scores(j, s_ref):
        kb = key_block(j)
        for g in range(GROUP):
            s_ref[g] = jnp.dot(kb, qt_ref[0, g], preferred_element_type=F32)

    def accumulate(j, s_ref):
        vb = value_block(j)
        for g in range(GROUP):
            p = jnp.exp2(s_ref[g] - m_ref[g])
            l_ref[g] += jnp.sum(p, axis=0, keepdims=True)
            acc_ref[g] += jnp.dot(vb, p.astype(BF16), preferred_element_type=F32)

    def online_step(j, carry):
        kb, vb = key_block(j), value_block(j)
        for g in range(GROUP):
            s = jnp.dot(kb, qt_ref[0, g], preferred_element_type=F32)
            m_old = m_ref[g]
            m_new = jnp.maximum(m_old, jnp.max(s, axis=0, keepdims=True))
            alpha = jnp.exp2(m_old - m_new)
            p = jnp.exp2(s - m_new)
            l_ref[g] = alpha * l_ref[g] + jnp.sum(p, axis=0, keepdims=True)
            acc_ref[g] = alpha * acc_ref[g] + jnp.dot(vb, p.astype(BF16), preferred_element_type=F32)
            m_ref[g] = m_new
        return carry

    @pl.when(shift_is_safe)
    def _():
        bufs = (s0_ref, s1_ref)
        scores(0, s0_ref)

        def group(t, carry):
            for u in range(KV_UNROLL):
                j = KV_UNROLL * t + u
                scores(j + 1, bufs[(u + 1) % 2])
                accumulate(j, bufs[u % 2])
            return carry

        lax.fori_loop(0, n_kv // KV_UNROLL - 1, group, 0)
        for u in range(KV_UNROLL):
            j = n_kv - KV_UNROLL + u
            if u + 1 < KV_UNROLL:
                scores(j + 1, bufs[(u + 1) % 2])
            accumulate(j, bufs[u % 2])

    @pl.when(jnp.logical_not(shift_is_safe))
    def _():
        m_ref[...] = jnp.full(m_ref.shape, NEG_BIG, F32)
        lax.fori_loop(0, n_kv, online_step, 0)

    for g in range(GROUP):
        o = acc_ref[g] / l_ref[g]
        o_ref[0, :, g * HEAD_DIM:(g + 1) * HEAD_DIM] = o.T.astype(o_ref.dtype)


def _attn(qt, k, vt):
    b, _, _, s = qt.shape
    tq = _tile(s, 512)
    tk = _tile(s // KV_UNROLL, 512)
    assert (s // tk) % KV_UNROLL == 0
    return pl.pallas_call(
        functools.partial(_attn_body, tk=tk),
        out_shape=jax.ShapeDtypeStruct((b, s, ATTN_W), BF16),
        grid=(b, N_KV_HEADS, s // tq),
        in_specs=[
            pl.BlockSpec((1, GROUP, HEAD_DIM, tq), lambda bi, kv, i: (bi, kv, 0, i)),
            pl.BlockSpec((1, 1, s, HEAD_DIM), lambda bi, kv, i: (bi, kv, 0, 0)),
            pl.BlockSpec((1, 1, HEAD_DIM, s), lambda bi, kv, i: (bi, kv, 0, 0)),
        ],
        out_specs=pl.BlockSpec((1, tq, GROUP * HEAD_DIM), lambda bi, kv, i: (bi, i, kv)),
        scratch_shapes=[
            pltpu.VMEM((1, 1), F32),
            pltpu.VMEM((GROUP, 1, tq), F32),
            pltpu.VMEM((GROUP, 1, tq), F32),
            pltpu.VMEM((GROUP, HEAD_DIM, tq), F32),
            pltpu.VMEM((GROUP, tk, tq), F32),
            pltpu.VMEM((GROUP, tk, tq), F32),
        ],
        compiler_params=_params(("parallel", "parallel", "arbitrary")),
        name="attn",
    )(qt, k, vt)


def _expm1_nonpos(x):
    e = jnp.exp(x)
    return jnp.where(e == 1.0, x, (e - 1.0) * x / jnp.log(e))


def _gelu_tanh(y):
    return 0.5 * y * (1.0 + jnp.tanh(math.sqrt(2.0 / math.pi) * (y + 0.044715 * (y * y * y))))


def _sigmoid(x):
    return 0.5 * jnp.tanh(0.5 * x) + 0.5


def _lru_scan(uc, wg_ref, bg_ref, nsp_ref, a_ref, b_ref, carry_ref, *, reverse):
    ts = uc.shape[0]

    @pl.when(pl.program_id(1) == 0)
    def _():
        carry_ref[...] = jnp.zeros(carry_ref.shape, F32)

    ucb = uc.astype(BF16)
    for hb in range(LRU_BLOCKS):
        cols = slice(hb * LRU_BW, (hb + 1) * LRU_BW)
        gates = jnp.dot(ucb[:, cols], wg_ref[hb], preferred_element_type=F32) + bg_ref[hb]
        r = _sigmoid(gates[:, :LRU_BW])
        ig = _sigmoid(gates[:, LRU_BW:])
        log_a = nsp_ref[:, cols] * r
        a_ref[:, cols] = jnp.exp(log_a)
        b_ref[:, cols] = jnp.sqrt(-_expm1_nonpos(2.0 * log_a)) * ig * uc[:, cols]

    nb = ts // SUBLANES
    row = lax.broadcasted_iota(jnp.int32, (SUBLANES, LRU_W), 0)

    def block(jb, carry):
        r0 = pl.multiple_of((nb - 1 - jb if reverse else jb) * SUBLANES, SUBLANES)
        a = a_ref[pl.ds(r0, SUBLANES), :]
        b = b_ref[pl.ds(r0, SUBLANES), :]
        for dist in (1, 2, 4):
            if reverse:
                shift, valid = SUBLANES - dist, row < SUBLANES - dist
            else:
                shift, valid = dist, row >= dist
            a_s = jnp.where(valid, pltpu.roll(a, shift, 0), 1.0)
            b_s = jnp.where(valid, pltpu.roll(b, shift, 0), 0.0)
            b = a * b_s + b
            a = a * a_s
        h = a * carry + b
        b_ref[pl.ds(r0, SUBLANES), :] = h
        return h[0:1, :] if reverse else h[SUBLANES - 1:SUBLANES, :]

    carry_ref[...] = lax.fori_loop(0, nb, block, carry_ref[...], unroll=4)


def _lru_fwd_body(u_ref, up_ref, un_ref, cw_ref, cb_ref, wg_ref, bg_ref, nsp_ref,
                  h_ref, uc_ref, ext_ref, a_ref, b_ref, carry_ref):
    ts = u_ref.shape[1]
    i = pl.program_id(1)
    ext_ref[0:SUBLANES, :] = jnp.where(i > 0, up_ref[0], 0.0)
    ext_ref[SUBLANES:SUBLANES + ts, :] = u_ref[0]
    ext_ref[SUBLANES + ts:2 * SUBLANES + ts, :] = jnp.where(i < pl.num_programs(1) - 1, un_ref[0], 0.0)
    ext = ext_ref[...]
    uc = cb_ref[...]
    for j in range(CONV_W):
        shift = (CONV_LEFT - j) % ext.shape[0]
        tap = pltpu.roll(ext, shift, 0) if shift else ext
        uc = uc + tap[SUBLANES:SUBLANES + ts, :] * cw_ref[j:j + 1, :]
    uc_ref[0] = uc
    _lru_scan(uc, wg_ref, bg_ref, nsp_ref, a_ref, b_ref, carry_ref, reverse=False)
    h_ref[0] = b_ref[...]


def _lru_bwd_body(uc_ref, wg_ref, bg_ref, nsp_ref, hf_ref, y_ref, o_ref, a_ref, b_ref, carry_ref):
    _lru_scan(uc_ref[0], wg_ref, bg_ref, nsp_ref, a_ref, b_ref, carry_ref, reverse=True)
    o_ref[0] = ((hf_ref[0] + b_ref[...]) * _gelu_tanh(y_ref[0])).astype(o_ref.dtype)


def _lru(u, y, cw, cb, wg, bg, nsp):
    b, s, w = u.shape
    ts = _tile(s, 512)
    n = s // ts
    nb8 = ts // SUBLANES

    def whole(a):
        return pl.BlockSpec(a.shape, lambda bi, i: (0,) * a.ndim)

    fwd_tile = pl.BlockSpec((1, ts, w), lambda bi, i: (bi, i, 0))
    bwd_tile = pl.BlockSpec((1, ts, w), lambda bi, i: (bi, n - 1 - i, 0))
    scan_scratch = [pltpu.VMEM((ts, w), F32), pltpu.VMEM((ts, w), F32), pltpu.VMEM((1, w), F32)]
    h_f, uc = pl.pallas_call(
        _lru_fwd_body,
        out_shape=(jax.ShapeDtypeStruct((b, s, w), F32), jax.ShapeDtypeStruct((b, s, w), F32)),
        grid=(b, n),
        in_specs=[
            fwd_tile,
            pl.BlockSpec((1, SUBLANES, w), lambda bi, i: (bi, jnp.maximum(i * nb8 - 1, 0), 0)),
            pl.BlockSpec((1, SUBLANES, w), lambda bi, i: (bi, jnp.minimum((i + 1) * nb8, s // SUBLANES - 1), 0)),
            whole(cw), whole(cb), whole(wg[0]), whole(bg[0]), whole(nsp[0:1]),
        ],
        out_specs=(fwd_tile, fwd_tile),
        scratch_shapes=[pltpu.VMEM((ts + 2 * SUBLANES, w), F32)] + scan_scratch,
        compiler_params=_params(("parallel", "arbitrary")),
        name="lru_fwd",
    )(u, u, u, cw, cb, wg[0], bg[0], nsp[0:1])
    return pl.pallas_call(
        _lru_bwd_body,
        out_shape=jax.ShapeDtypeStruct((b, s, w), BF16),
        grid=(b, n),
        in_specs=[bwd_tile, whole(wg[1]), whole(bg[1]), whole(nsp[1:2]), bwd_tile, bwd_tile],
        out_specs=bwd_tile,
        scratch_shapes=scan_scratch,
        compiler_params=_params(("parallel", "arbitrary")),
        name="lru_bwd",
    )(uc, wg[1], bg[1], nsp[1:2], h_f, y)


def _outproj_body(x_ref, a_ref, r_ref, wa_ref, wr_ref, o_ref):
    o_ref[...] = (x_ref[...]
                  + jnp.dot(a_ref[...], wa_ref[...], preferred_element_type=F32)
                  + jnp.dot(r_ref[...], wr_ref[...], preferred_element_type=F32))


def _outproj(x, attn, lru, wa, wr):
    t, d = x.shape
    tm = _tile(t, 512)
    return pl.pallas_call(
        _outproj_body,
        out_shape=jax.ShapeDtypeStruct((t, d), F32),
        grid=(t // tm,),
        in_specs=[
            pl.BlockSpec((tm, d), lambda i: (i, 0)),
            pl.BlockSpec((tm, attn.shape[1]), lambda i: (i, 0)),
            pl.BlockSpec((tm, lru.shape[1]), lambda i: (i, 0)),
            _resident(wa.shape),
            _resident(wr.shape),
        ],
        out_specs=pl.BlockSpec((tm, d), lambda i: (i, 0)),
        compiler_params=_params(("parallel",)),
        name="outproj",
    )(x, attn, lru, wa, wr)


def _ple_body(x_ref, p_ref, g_ref, wg_ref, wp_ref, gf_ref, o_ref, *, final):
    x = x_ref[...]
    h = _rms(x, g_ref[...]).astype(BF16)
    gate = jax.nn.sigmoid(jnp.dot(h, wg_ref[...], preferred_element_type=F32))
    emb = jnp.dot(p_ref[...].astype(BF16), wp_ref[...], preferred_element_type=F32)
    x = x + gate * emb
    o_ref[...] = _rms(x, gf_ref[...]) if final else x


def _ple(x, p, g, wg, wp, gf, *, final):
    t, d = x.shape
    tm = _tile(t, 512)
    return pl.pallas_call(
        functools.partial(_ple_body, final=final),
        out_shape=jax.ShapeDtypeStruct((t, d), F32),
        grid=(t // tm,),
        in_specs=[
            pl.BlockSpec((tm, d), lambda i: (i, 0)),
            pl.BlockSpec((tm, p.shape[1]), lambda i: (i, 0)),
            pl.BlockSpec((1, d), lambda i: (0, 0)),
            _resident(wg.shape),
            _resident(wp.shape),
            pl.BlockSpec((1, d), lambda i: (0, 0)),
        ],
        out_specs=pl.BlockSpec((tm, d), lambda i: (i, 0)),
        compiler_params=_params(("parallel",)),
        name="ple",
    )(x, p, g, wg, wp, gf)


def _rope_tables(seq_len):
    rows = seq_len // GRID_W
    inv = ROPE_THETA ** (-jnp.arange(0, AXIS_DIM, 2, dtype=F32) / AXIS_DIM)
    row_ang = jnp.arange(rows, dtype=F32)[:, None] * inv
    col_ang = jnp.arange(GRID_W, dtype=F32)[:, None] * inv

    def expand(fn):
        return jnp.concatenate([jnp.repeat(fn(row_ang), GRID_W, axis=0), jnp.tile(fn(col_ang), (rows, 1))], axis=-1)

    cos, sin = expand(jnp.cos), expand(jnp.sin)
    return jnp.concatenate([cos, cos], axis=-1), jnp.concatenate([-sin, sin], axis=-1)


def _deinterleave_perm():
    half = jnp.concatenate([jnp.arange(0, HEAD_DIM, 2), jnp.arange(1, HEAD_DIM, 2)])
    heads = jnp.arange(N_Q_HEADS + N_KV_HEADS)[:, None] * HEAD_DIM
    return (heads + half[None, :]).reshape(-1)


def kernel(x, p, norm_ffn1, w1_ffn1, w3_ffn1, w2_ffn1, norm_mix, w_in, q_norm, k_norm, conv_w, conv_b,
           lru_wa, lru_ba, lru_wi, lru_bi, lru_lambda, w_out, norm_ffn2, w1_ffn2, w3_ffn2, w2_ffn2,
           norm_ple, w_ple_gate, w_ple_proj, norm_final):
    b, s, d = x.shape
    depth = w_in.shape[0]
    t = b * s
    cos, sin = _rope_tables(s)
    perm = _deinterleave_perm()
    half = perm[:HEAD_DIM]
    qk_w = ATTN_W + KV_W

    x = x.reshape(t, d)
    for l in range(depth):
        row = lambda v: v.reshape(1, -1)
        x = _ffn(x, row(norm_ffn1[l]), w1_ffn1[l].astype(BF16), w3_ffn1[l].astype(BF16), w2_ffn1[l].astype(BF16))

        w_in_l = w_in[l].astype(BF16)
        w_in_l = jnp.concatenate([w_in_l[:, :qk_w][:, perm], w_in_l[:, qk_w:]], axis=1)
        qt, k, vt, u, y = _inproj(x.reshape(b, s, d), row(norm_mix[l]), w_in_l,
                                  row(q_norm[l][half]), row(k_norm[l][half]), cos, sin)
        attn = _attn(qt, k, vt)

        nsp = -LRU_C * jax.nn.softplus(-lru_lambda[l])
        wg = jnp.concatenate([lru_wa[l], lru_wi[l]], axis=-1).astype(BF16)
        bg = jnp.concatenate([lru_ba[l], lru_bi[l]], axis=-1)[:, :, None, :]
        lru = _lru(u, y, conv_w[l], row(conv_b[l]), wg, bg, nsp)

        w_out_l = w_out[l].astype(BF16)
        x = _outproj(x, attn.reshape(t, ATTN_W), lru.reshape(t, LRU_W), w_out_l[:ATTN_W], w_out_l[ATTN_W:])

        x = _ffn(x, row(norm_ffn2[l]), w1_ffn2[l].astype(BF16), w3_ffn2[l].astype(BF16), w2_ffn2[l].astype(BF16))

        x = _ple(x, p[l].reshape(t, -1), row(norm_ple[l]), w_ple_gate[l].astype(BF16),
                 w_ple_proj[l].astype(BF16), row(norm_final), final=(l == depth - 1))
    return x.reshape(b, s, d)
```

```python
import functools
import math

import jax
import jax.numpy as jnp
from jax import lax
from jax.experimental import pallas as pl
from jax.experimental.pallas import tpu as pltpu

F32 = jnp.float32
BF16 = jnp.bfloat16

EPS = 1e-6
HEAD_DIM = 128
N_Q_HEADS = 8
N_KV_HEADS = 2
GROUP = N_Q_HEADS // N_KV_HEADS
ATTN_W = N_Q_HEADS * HEAD_DIM
KV_W = N_KV_HEADS * HEAD_DIM
LRU_BLOCKS = 8
LRU_BW = 128
LRU_W = LRU_BLOCKS * LRU_BW
LRU_C = 8.0
CONV_W = 4
CONV_LEFT = 2
GRID_W = 64
ROPE_THETA = 10000.0
AXIS_DIM = HEAD_DIM // 2

V7X_VMEM_BYTES = 64 * 1024 * 1024
VMEM_LIMIT_BYTES = V7X_VMEM_BYTES - 8 * 1024 * 1024
SUBLANES = 8
Q_SCALE = math.log2(math.e) / math.sqrt(HEAD_DIM)
NEG_BIG = -1e30
SAFE_SHIFT_LOG2 = 60.0
FFN_PARTS = 2
KV_UNROLL = 8


def _tile(n, pref):
    t = pref
    while n % t:
        t //= 2
    return t


def _params(sem):
    return pltpu.CompilerParams(dimension_semantics=sem, vmem_limit_bytes=VMEM_LIMIT_BYTES)


def _rms(x, g):
    return x * lax.rsqrt(jnp.mean(x * x, axis=-1, keepdims=True) + EPS) * g


def _resident(shape):
    nd = len(shape)
    return pl.BlockSpec(shape, lambda *_: (0,) * nd, pipeline_mode=pl.Buffered(1))


def _ffn_part_body(*refs, first, col_chunk):
    if first:
        y_ref, g_ref, w1_ref, w3_ref, w2_ref, o_ref, h_ref = refs
        h = _rms(y_ref[...], g_ref[...]).astype(BF16)
        h_ref[...] = h
    else:
        h_ref, y_ref, w1_ref, w3_ref, w2_ref, o_ref = refs
        h = h_ref[...]
    o_ref[...] = y_ref[...]
    width = w1_ref.shape[1]
    for c0 in range(0, width, col_chunk):
        cols = slice(c0, min(c0 + col_chunk, width))
        a = jnp.dot(h, w1_ref[:, cols], preferred_element_type=F32)
        b = jnp.dot(h, w3_ref[:, cols], preferred_element_type=F32)
        z = (0.5 * a * jax.nn.sigmoid(a) * b).astype(BF16)
        o_ref[...] += jnp.dot(z, w2_ref[cols, :], preferred_element_type=F32)


def _ffn(x, g, w1, w3, w2):
    t, d = x.shape
    width = w1.shape[1] // FFN_PARTS
    tm = _tile(t, 256)
    tile = pl.BlockSpec((tm, d), lambda i: (i, 0))
    y, h = x, None
    for part in range(FFN_PARTS):
        first = part == 0
        weights = [
            pl.BlockSpec((d, width), lambda i, part=part: (0, part), pipeline_mode=pl.Buffered(1)),
            pl.BlockSpec((d, width), lambda i, part=part: (0, part), pipeline_mode=pl.Buffered(1)),
            pl.BlockSpec((width, d), lambda i, part=part: (part, 0), pipeline_mode=pl.Buffered(1)),
        ]
        res = jax.ShapeDtypeStruct((t, d), F32)
        out = pl.pallas_call(
            functools.partial(_ffn_part_body, first=first, col_chunk=1024),
            out_shape=(res, jax.ShapeDtypeStruct((t, d), BF16)) if first else res,
            grid=(t // tm,),
            in_specs=([tile, pl.BlockSpec((1, d), lambda i: (0, 0))] if first else [tile, tile]) + weights,
            out_specs=(tile, tile) if first else tile,
            compiler_params=_params(("parallel",)),
            name="ffn",
        )(*((y, g) if first else (h, y)), w1, w3, w2)
        y, h = out if first else (out, h)
    return y


def _inproj_body(x_ref, g_ref, w_ref, gq_ref, gk_ref, cos_ref, sin_ref,
                 qt_ref, k_ref, vt_ref, u_ref, y_ref):
    h = _rms(x_ref[0], g_ref[...]).astype(BF16)
    proj = jnp.dot(h, w_ref[...], preferred_element_type=F32)
    cos = cos_ref[...]
    sin = sin_ref[...]

    def norm_rope(z, g):
        z = _rms(z, g)
        return z * cos + pltpu.roll(z, HEAD_DIM // 2, 1) * sin

    for hd in range(N_Q_HEADS):
        q = norm_rope(proj[:, hd * HEAD_DIM:(hd + 1) * HEAD_DIM], gq_ref[...]) * Q_SCALE
        qt_ref[0, hd] = q.T.astype(BF16)
    for kv in range(N_KV_HEADS):
        c0 = ATTN_W + kv * HEAD_DIM
        k_ref[0, kv] = norm_rope(proj[:, c0:c0 + HEAD_DIM], gk_ref[...]).astype(BF16)
        c0 = ATTN_W + KV_W + kv * HEAD_DIM
        vt_ref[0, kv] = proj[:, c0:c0 + HEAD_DIM].T.astype(BF16)
    c0 = ATTN_W + 2 * KV_W
    u_ref[0] = proj[:, c0:c0 + LRU_W]
    y_ref[0] = proj[:, c0 + LRU_W:c0 + 2 * LRU_W]


def _inproj(x, g, w, gq, gk, cos, sin):
    b, s, d = x.shape
    tm = _tile(s, 512)
    n = s // tm
    return pl.pallas_call(
        _inproj_body,
        out_shape=(
            jax.ShapeDtypeStruct((b, N_Q_HEADS, HEAD_DIM, s), BF16),
            jax.ShapeDtypeStruct((b, N_KV_HEADS, s, HEAD_DIM), BF16),
            jax.ShapeDtypeStruct((b, N_KV_HEADS, HEAD_DIM, s), BF16),
            jax.ShapeDtypeStruct((b, s, LRU_W), F32),
            jax.ShapeDtypeStruct((b, s, LRU_W), F32),
        ),
        grid=(b, n),
        in_specs=[
            pl.BlockSpec((1, tm, d), lambda bi, i: (bi, i, 0)),
            pl.BlockSpec((1, d), lambda bi, i: (0, 0)),
            _resident(w.shape),
            pl.BlockSpec((1, HEAD_DIM), lambda bi, i: (0, 0)),
            pl.BlockSpec((1, HEAD_DIM), lambda bi, i: (0, 0)),
            pl.BlockSpec((tm, HEAD_DIM), lambda bi, i: (i, 0)),
            pl.BlockSpec((tm, HEAD_DIM), lambda bi, i: (i, 0)),
        ],
        out_specs=(
            pl.BlockSpec((1, N_Q_HEADS, HEAD_DIM, tm), lambda bi, i: (bi, 0, 0, i)),
            pl.BlockSpec((1, N_KV_HEADS, tm, HEAD_DIM), lambda bi, i: (bi, 0, i, 0)),
            pl.BlockSpec((1, N_KV_HEADS, HEAD_DIM, tm), lambda bi, i: (bi, 0, 0, i)),
            pl.BlockSpec((1, tm, LRU_W), lambda bi, i: (bi, i, 0)),
            pl.BlockSpec((1, tm, LRU_W), lambda bi, i: (bi, i, 0)),
        ),
        compiler_params=_params(("parallel", "parallel")),
        name="inproj",
    )(x, g, w, gq, gk, cos, sin)


def _attn_body(qt_ref, k_ref, vt_ref, o_ref, kmax_ref, m_ref, l_ref, acc_ref, s0_ref, s1_ref, *, tk):
    s_len = k_ref.shape[2]
    n_kv = s_len // tk

    @pl.when(pl.program_id(2) == 0)
    def _():
        def key_chunk(j, best):
            kc = k_ref[0, 0, pl.ds(pl.multiple_of(j * tk, tk), tk), :].astype(F32)
            return jnp.maximum(best, jnp.sum(kc * kc, axis=1, keepdims=True))
        best = lax.fori_loop(0, n_kv, key_chunk, jnp.zeros((tk, 1), F32))
        kmax_ref[...] = jnp.max(best, axis=0, keepdims=True)

    for g in range(GROUP):
        qf = qt_ref[0, g].astype(F32)
        m_ref[g] = jnp.sqrt(jnp.sum(qf * qf, axis=0, keepdims=True) * kmax_ref[...])
    l_ref[...] = jnp.zeros(l_ref.shape, F32)
    acc_ref[...] = jnp.zeros(acc_ref.shape, F32)
    shift_is_safe = jnp.max(m_ref[...]) <= SAFE_SHIFT_LOG2

    def key_block(j):
        return k_ref[0, 0, pl.ds(pl.multiple_of(j * tk, tk), tk), :]

    def value_block(j):
        return vt_ref[0, 0, :, pl.ds(pl.multiple_of(j * tk, tk), tk)]

    def scores(j, s_ref):
        kb = key_block(j)
        for g in range(GROUP):
            s_ref[g] = jnp.dot(kb, qt_ref[0, g], preferred_element_type=F32)

    def accumulate(j, s_ref):
        vb = value_block(j)
        for g in range(GROUP):
            p = jnp.exp2(s_ref[g] - m_ref[g])
            l_ref[g] += jnp.sum(p, axis=0, keepdims=True)
            acc_ref[g] += jnp.dot(vb, p.astype(BF16), preferred_element_type=F32)

    def online_step(j, carry):
        kb, vb = key_block(j), value_block(j)
        for g in range(GROUP):
            s = jnp.dot(kb, qt_ref[0, g], preferred_element_type=F32)
            m_old = m_ref[g]
            m_new = jnp.maximum(m_old, jnp.max(s, axis=0, keepdims=True))
            alpha = jnp.exp2(m_old - m_new)
            p = jnp.exp2(s - m_new)
            l_ref[g] = alpha * l_ref[g] + jnp.sum(p, axis=0, keepdims=True)
            acc_ref[g] = alpha * acc_ref[g] + jnp.dot(vb, p.astype(BF16), preferred_element_type=F32)
            m_ref[g] = m_new
        return carry

    @pl.when(shift_is_safe)
    def _():
        bufs = (s0_ref, s1_ref)
        scores(0, s0_ref)

        def group(t, carry):
            for u in range(KV_UNROLL):
                j = KV_UNROLL * t + u
                scores(j + 1, bufs[(u + 1) % 2])
                accumulate(j, bufs[u % 2])
            return carry

        lax.fori_loop(0, n_kv // KV_UNROLL - 1, group, 0)
        for u in range(KV_UNROLL):
            j = n_kv - KV_UNROLL + u
            if u + 1 < KV_UNROLL:
                scores(j + 1, bufs[(u + 1) % 2])
            accumulate(j, bufs[u % 2])

    @pl.when(jnp.logical_not(shift_is_safe))
    def _():
        m_ref[...] = jnp.full(m_ref.shape, NEG_BIG, F32)
        lax.fori_loop(0, n_kv, online_step, 0)

    for g in range(GROUP):
        o = acc_ref[g] / l_ref[g]
        o_ref[0, :, g * HEAD_DIM:(g + 1) * HEAD_DIM] = o.T.astype(o_ref.dtype)


def _attn(qt, k, vt):
    b, _, _, s = qt.shape
    tq = _tile(s, 512)
    tk = _tile(s // KV_UNROLL, 512)
    assert (s // tk) % KV_UNROLL == 0
    return pl.pallas_call(
        functools.partial(_attn_body, tk=tk),
        out_shape=jax.ShapeDtypeStruct((b, s, ATTN_W), BF16),
        grid=(b, N_KV_HEADS, s // tq),
        in_specs=[
            pl.BlockSpec((1, GROUP, HEAD_DIM, tq), lambda bi, kv, i: (bi, kv, 0, i)),
            pl.BlockSpec((1, 1, s, HEAD_DIM), lambda bi, kv, i: (bi, kv, 0, 0)),
            pl.BlockSpec((1, 1, HEAD_DIM, s), lambda bi, kv, i: (bi, kv, 0, 0)),
        ],
        out_specs=pl.BlockSpec((1, tq, GROUP * HEAD_DIM), lambda bi, kv, i: (bi, i, kv)),
        scratch_shapes=[
            pltpu.VMEM((1, 1), F32),
            pltpu.VMEM((GROUP, 1, tq), F32),
            pltpu.VMEM((GROUP, 1, tq), F32),
            pltpu.VMEM((GROUP, HEAD_DIM, tq), F32),
            pltpu.VMEM((GROUP, tk, tq), F32),
            pltpu.VMEM((GROUP, tk, tq), F32),
        ],
        compiler_params=_params(("parallel", "parallel", "arbitrary")),
        name="attn",
    )(qt, k, vt)


def _expm1_nonpos(x):
    e = jnp.exp(x)
    return jnp.where(e == 1.0, x, (e - 1.0) * x / jnp.log(e))


def _gelu_tanh(y):
    return 0.5 * y * (1.0 + jnp.tanh(math.sqrt(2.0 / math.pi) * (y + 0.044715 * (y * y * y))))


def _sigmoid(x):
    return 0.5 * jnp.tanh(0.5 * x) + 0.5


def _seg_pitch(ts):
    return ts // SUBLANES + SUBLANES // 2


def _lru_scan(uc, wg_ref, bg_ref, nsp_ref, a_ref, b_ref, carry_ref, *, reverse):
    ts = uc.shape[0]
    seg = ts // SUBLANES
    pitch = _seg_pitch(ts)

    @pl.when(pl.program_id(1) == 0)
    def _():
        carry_ref[...] = jnp.zeros(carry_ref.shape, F32)

    ucb = uc.astype(BF16)
    for hb in range(LRU_BLOCKS):
        cols = slice(hb * LRU_BW, (hb + 1) * LRU_BW)
        gates = jnp.dot(ucb[:, cols], wg_ref[hb], preferred_element_type=F32) + bg_ref[hb]
        r = _sigmoid(gates[:, :LRU_BW])
        ig = _sigmoid(gates[:, LRU_BW:])
        log_a = nsp_ref[:, cols] * r
        a = jnp.exp(log_a)
        b = jnp.sqrt(-_expm1_nonpos(2.0 * log_a)) * ig * uc[:, cols]
        for sg in range(SUBLANES):
            a_ref[hb, pitch * sg:pitch * sg + seg, :] = a[seg * sg:seg * (sg + 1), :]
            b_ref[hb, pitch * sg:pitch * sg + seg, :] = b[seg * sg:seg * (sg + 1), :]

    def rows_at(j):
        return pl.ds(seg - 1 - j if reverse else j, SUBLANES, stride=pitch)

    def local_step(j, state):
        hs, ps = state
        new_h, new_p = [], []
        for hb in range(LRU_BLOCKS):
            a = a_ref[hb, rows_at(j), :]
            new_h.append(a * hs[hb] + b_ref[hb, rows_at(j), :])
            new_p.append(a * ps[hb])
        return tuple(new_h), tuple(new_p)

    zeros = tuple(jnp.zeros((SUBLANES, LRU_BW), F32) for _ in range(LRU_BLOCKS))
    ones = tuple(jnp.ones((SUBLANES, LRU_BW), F32) for _ in range(LRU_BLOCKS))
    h_end, p_end = lax.fori_loop(0, seg, local_step, (zeros, ones), unroll=2)

    row = lax.broadcasted_iota(jnp.int32, (SUBLANES, LRU_BW), 0)
    starts = []
    for hb in range(LRU_BLOCKS):
        cols = slice(hb * LRU_BW, (hb + 1) * LRU_BW)
        a, b = p_end[hb], h_end[hb]
        for dist in (1, 2, 4):
            if reverse:
                shift, valid = SUBLANES - dist, row < SUBLANES - dist
            else:
                shift, valid = dist, row >= dist
            a_s = jnp.where(valid, pltpu.roll(a, shift, 0), 1.0)
            b_s = jnp.where(valid, pltpu.roll(b, shift, 0), 0.0)
            b = a * b_s + b
            a = a * a_s
        carry = carry_ref[:, cols]
        ends = a * carry + b
        if reverse:
            starts.append(jnp.where(row == SUBLANES - 1, carry, pltpu.roll(ends, SUBLANES - 1, 0)))
            carry_ref[:, cols] = ends[0:1, :]
        else:
            starts.append(jnp.where(row == 0, carry, pltpu.roll(ends, 1, 0)))
            carry_ref[:, cols] = ends[SUBLANES - 1:SUBLANES, :]

    def final_step(j, hs):
        new_h = []
        for hb in range(LRU_BLOCKS):
            h = a_ref[hb, rows_at(j), :] * hs[hb] + b_ref[hb, rows_at(j), :]
            b_ref[hb, rows_at(j), :] = h
            new_h.append(h)
        return tuple(new_h)

    lax.fori_loop(0, seg, final_step, tuple(starts), unroll=2)
    return jnp.concatenate(
        [jnp.concatenate([b_ref[hb, pitch * sg:pitch * sg + seg, :] for sg in range(SUBLANES)], axis=0)
         for hb in range(LRU_BLOCKS)], axis=1)


def _lru_fwd_body(u_ref, up_ref, un_ref, cw_ref, cb_ref, wg_ref, bg_ref, nsp_ref,
                  h_ref, uc_ref, ext_ref, a_ref, b_ref, carry_ref):
    ts = u_ref.shape[1]
    i = pl.program_id(1)
    ext_ref[0:SUBLANES, :] = jnp.where(i > 0, up_ref[0], 0.0)
    ext_ref[SUBLANES:SUBLANES + ts, :] = u_ref[0]
    ext_ref[SUBLANES + ts:2 * SUBLANES + ts, :] = jnp.where(i < pl.num_programs(1) - 1, un_ref[0], 0.0)
    ext = ext_ref[...]
    uc = cb_ref[...]
    for j in range(CONV_W):
        shift = (CONV_LEFT - j) % ext.shape[0]
        tap = pltpu.roll(ext, shift, 0) if shift else ext
        uc = uc + tap[SUBLANES:SUBLANES + ts, :] * cw_ref[j:j + 1, :]
    uc_ref[0] = uc
    h_ref[0] = _lru_scan(uc, wg_ref, bg_ref, nsp_ref, a_ref, b_ref, carry_ref, reverse=False)


def _lru_bwd_body(uc_ref, wg_ref, bg_ref, nsp_ref, hf_ref, y_ref, o_ref, a_ref, b_ref, carry_ref):
    h_b = _lru_scan(uc_ref[0], wg_ref, bg_ref, nsp_ref, a_ref, b_ref, carry_ref, reverse=True)
    o_ref[0] = ((hf_ref[0] + h_b) * _gelu_tanh(y_ref[0])).astype(o_ref.dtype)


def _lru(u, y, cw, cb, wg, bg, nsp):
    b, s, w = u.shape
    ts = _tile(s, 512)
    n = s // ts
    nb8 = ts // SUBLANES

    def whole(a):
        return pl.BlockSpec(a.shape, lambda bi, i: (0,) * a.ndim)

    fwd_tile = pl.BlockSpec((1, ts, w), lambda bi, i: (bi, i, 0))
    bwd_tile = pl.BlockSpec((1, ts, w), lambda bi, i: (bi, n - 1 - i, 0))
    seg_rows = SUBLANES * _seg_pitch(ts)
    scan_scratch = [pltpu.VMEM((LRU_BLOCKS, seg_rows, LRU_BW), F32), pltpu.VMEM((LRU_BLOCKS, seg_rows, LRU_BW), F32),
                    pltpu.VMEM((1, w), F32)]
    h_f, uc = pl.pallas_call(
        _lru_fwd_body,
        out_shape=(jax.ShapeDtypeStruct((b, s, w), F32), jax.ShapeDtypeStruct((b, s, w), F32)),
        grid=(b, n),
        in_specs=[
            fwd_tile,
            pl.BlockSpec((1, SUBLANES, w), lambda bi, i: (bi, jnp.maximum(i * nb8 - 1, 0), 0)),
            pl.BlockSpec((1, SUBLANES, w), lambda bi, i: (bi, jnp.minimum((i + 1) * nb8, s // SUBLANES - 1), 0)),
            whole(cw), whole(cb), whole(wg[0]), whole(bg[0]), whole(nsp[0:1]),
        ],
        out_specs=(fwd_tile, fwd_tile),
        scratch_shapes=[pltpu.VMEM((ts + 2 * SUBLANES, w), F32)] + scan_scratch,
        compiler_params=_params(("parallel", "arbitrary")),
        name="lru_fwd",
    )(u, u, u, cw, cb, wg[0], bg[0], nsp[0:1])
    return pl.pallas_call(
        _lru_bwd_body,
        out_shape=jax.ShapeDtypeStruct((b, s, w), BF16),
        grid=(b, n),
        in_specs=[bwd_tile, whole(wg[1]), whole(bg[1]), whole(nsp[1:2]), bwd_tile, bwd_tile],
        out_specs=bwd_tile,
        scratch_shapes=scan_scratch,
        compiler_params=_params(("parallel", "arbitrary")),
        name="lru_bwd",
    )(uc, wg[1], bg[1], nsp[1:2], h_f, y)


def _outproj_body(x_ref, a_ref, r_ref, wa_ref, wr_ref, o_ref):
    o_ref[...] = (x_ref[...]
                  + jnp.dot(a_ref[...], wa_ref[...], preferred_element_type=F32)
                  + jnp.dot(r_ref[...], wr_ref[...], preferred_element_type=F32))


def _outproj(x, attn, lru, wa, wr):
    t, d = x.shape
    tm = _tile(t, 512)
    return pl.pallas_call(
        _outproj_body,
        out_shape=jax.ShapeDtypeStruct((t, d), F32),
        grid=(t // tm,),
        in_specs=[
            pl.BlockSpec((tm, d), lambda i: (i, 0)),
            pl.BlockSpec((tm, attn.shape[1]), lambda i: (i, 0)),
            pl.BlockSpec((tm, lru.shape[1]), lambda i: (i, 0)),
            _resident(wa.shape),
            _resident(wr.shape),
        ],
        out_specs=pl.BlockSpec((tm, d), lambda i: (i, 0)),
        compiler_params=_params(("parallel",)),
        name="outproj",
    )(x, attn, lru, wa, wr)


def _ple_body(x_ref, p_ref, g_ref, wg_ref, wp_ref, gf_ref, o_ref, *, final):
    x = x_ref[...]
    h = _rms(x, g_ref[...]).astype(BF16)
    gate = jax.nn.sigmoid(jnp.dot(h, wg_ref[...], preferred_element_type=F32))
    emb = jnp.dot(p_ref[...].astype(BF16), wp_ref[...], preferred_element_type=F32)
    x = x + gate * emb
    o_ref[...] = _rms(x, gf_ref[...]) if final else x


def _ple(x, p, g, wg, wp, gf, *, final):
    t, d = x.shape
    tm = _tile(t, 512)
    return pl.pallas_call(
        functools.partial(_ple_body, final=final),
        out_shape=jax.ShapeDtypeStruct((t, d), F32),
        grid=(t // tm,),
        in_specs=[
            pl.BlockSpec((tm, d), lambda i: (i, 0)),
            pl.BlockSpec((tm, p.shape[1]), lambda i: (i, 0)),
            pl.BlockSpec((1, d), lambda i: (0, 0)),
            _resident(wg.shape),
            _resident(wp.shape),
            pl.BlockSpec((1, d), lambda i: (0, 0)),
        ],
        out_specs=pl.BlockSpec((tm, d), lambda i: (i, 0)),
        compiler_params=_params(("parallel",)),
        name="ple",
    )(x, p, g, wg, wp, gf)


def _rope_tables(seq_len):
    rows = seq_len // GRID_W
    inv = ROPE_THETA ** (-jnp.arange(0, AXIS_DIM, 2, dtype=F32) / AXIS_DIM)
    row_ang = jnp.arange(rows, dtype=F32)[:, None] * inv
    col_ang = jnp.arange(GRID_W, dtype=F32)[:, None] * inv

    def expand(fn):
        return jnp.concatenate([jnp.repeat(fn(row_ang), GRID_W, axis=0), jnp.tile(fn(col_ang), (rows, 1))], axis=-1)

    cos, sin = expand(jnp.cos), expand(jnp.sin)
    return jnp.concatenate([cos, cos], axis=-1), jnp.concatenate([-sin, sin], axis=-1)


def _deinterleave_perm():
    half = jnp.concatenate([jnp.arange(0, HEAD_DIM, 2), jnp.arange(1, HEAD_DIM, 2)])
    heads = jnp.arange(N_Q_HEADS + N_KV_HEADS)[:, None] * HEAD_DIM
    return (heads + half[None, :]).reshape(-1)


def kernel(x, p, norm_ffn1, w1_ffn1, w3_ffn1, w2_ffn1, norm_mix, w_in, q_norm, k_norm, conv_w, conv_b,
           lru_wa, lru_ba, lru_wi, lru_bi, lru_lambda, w_out, norm_ffn2, w1_ffn2, w3_ffn2, w2_ffn2,
           norm_ple, w_ple_gate, w_ple_proj, norm_final):
    b, s, d = x.shape
    depth = w_in.shape[0]
    t = b * s
    cos, sin = _rope_tables(s)
    perm = _deinterleave_perm()
    half = perm[:HEAD_DIM]
    qk_w = ATTN_W + KV_W

    x = x.reshape(t, d)
    for l in range(depth):
        row = lambda v: v.reshape(1, -1)
        x = _ffn(x, row(norm_ffn1[l]), w1_ffn1[l].astype(BF16), w3_ffn1[l].astype(BF16), w2_ffn1[l].astype(BF16))

        w_in_l = w_in[l].astype(BF16)
        w_in_l = jnp.concatenate([w_in_l[:, :qk_w][:, perm], w_in_l[:, qk_w:]], axis=1)
        qt, k, vt, u, y = _inproj(x.reshape(b, s, d), row(norm_mix[l]), w_in_l,
                                  row(q_norm[l][half]), row(k_norm[l][half]), cos, sin)
        attn = _attn(qt, k, vt)

        nsp = -LRU_C * jax.nn.softplus(-lru_lambda[l])
        wg = jnp.concatenate([lru_wa[l], lru_wi[l]], axis=-1).astype(BF16)
        bg = jnp.concatenate([lru_ba[l], lru_bi[l]], axis=-1)[:, :, None, :]
        lru = _lru(u, y, conv_w[l], row(conv_b[l]), wg, bg, nsp)

        w_out_l = w_out[l].astype(BF16)
        x = _outproj(x, attn.reshape(t, ATTN_W), lru.reshape(t, LRU_W), w_out_l[:ATTN_W], w_out_l[ATTN_W:])

        x = _ffn(x, row(norm_ffn2[l]), w1_ffn2[l].astype(BF16), w3_ffn2[l].astype(BF16), w2_ffn2[l].astype(BF16))

        x = _ple(x, p[l].reshape(t, -1), row(norm_ple[l]), w_ple_gate[l].astype(BF16),
                 w_ple_proj[l].astype(BF16), row(norm_final), final=(l == depth - 1))
    return x.reshape(b, s, d)
```

```python
import functools
import math

import jax
import jax.numpy as jnp
from jax import lax
from jax.experimental import pallas as pl
from jax.experimental.pallas import tpu as pltpu

F32 = jnp.float32
BF16 = jnp.bfloat16

EPS = 1e-6
HEAD_DIM = 128
N_Q_HEADS = 8
N_KV_HEADS = 2
GROUP = N_Q_HEADS // N_KV_HEADS
ATTN_W = N_Q_HEADS * HEAD_DIM
KV_W = N_KV_HEADS * HEAD_DIM
LRU_BLOCKS = 8
LRU_BW = 128
LRU_W = LRU_BLOCKS * LRU_BW
LRU_C = 8.0
CONV_W = 4
CONV_LEFT = 2
GRID_W = 64
ROPE_THETA = 10000.0
AXIS_DIM = HEAD_DIM // 2

V7X_VMEM_BYTES = 64 * 1024 * 1024
VMEM_LIMIT_BYTES = V7X_VMEM_BYTES - 8 * 1024 * 1024
SUBLANES = 8
Q_SCALE = math.log2(math.e) / math.sqrt(HEAD_DIM)
NEG_BIG = -1e30
SAFE_SHIFT_LOG2 = 60.0
FFN_PARTS = 2
KV_UNROLL = 8
SCAN_UNROLL = 8


def _tile(n, pref):
    t = pref
    while n % t:
        t //= 2
    return t


def _params(sem):
    return pltpu.CompilerParams(dimension_semantics=sem, vmem_limit_bytes=VMEM_LIMIT_BYTES)


def _rms(x, g):
    return x * lax.rsqrt(jnp.mean(x * x, axis=-1, keepdims=True) + EPS) * g


def _resident(shape):
    nd = len(shape)
    return pl.BlockSpec(shape, lambda *_: (0,) * nd, pipeline_mode=pl.Buffered(1))


def _ffn_part_body(*refs, first, col_chunk):
    if first:
        y_ref, g_ref, w1_ref, w3_ref, w2_ref, o_ref, h_ref = refs
        h = _rms(y_ref[...], g_ref[...]).astype(BF16)
        h_ref[...] = h
    else:
        h_ref, y_ref, w1_ref, w3_ref, w2_ref, o_ref = refs
        h = h_ref[...]
    o_ref[...] = y_ref[...]
    width = w1_ref.shape[1]
    for c0 in range(0, width, col_chunk):
        cols = slice(c0, min(c0 + col_chunk, width))
        a = jnp.dot(h, w1_ref[:, cols], preferred_element_type=F32)
        b = jnp.dot(h, w3_ref[:, cols], preferred_element_type=F32)
        z = (0.5 * a * jax.nn.sigmoid(a) * b).astype(BF16)
        o_ref[...] += jnp.dot(z, w2_ref[cols, :], preferred_element_type=F32)


def _ffn(x, g, w1, w3, w2):
    t, d = x.shape
    width = w1.shape[1] // FFN_PARTS
    tm = _tile(t, 256)
    tile = pl.BlockSpec((tm, d), lambda i: (i, 0))
    y, h = x, None
    for part in range(FFN_PARTS):
        first = part == 0
        weights = [
            pl.BlockSpec((d, width), lambda i, part=part: (0, part), pipeline_mode=pl.Buffered(1)),
            pl.BlockSpec((d, width), lambda i, part=part: (0, part), pipeline_mode=pl.Buffered(1)),
            pl.BlockSpec((width, d), lambda i, part=part: (part, 0), pipeline_mode=pl.Buffered(1)),
        ]
        res = jax.ShapeDtypeStruct((t, d), F32)
        out = pl.pallas_call(
            functools.partial(_ffn_part_body, first=first, col_chunk=1024),
            out_shape=(res, jax.ShapeDtypeStruct((t, d), BF16)) if first else res,
            grid=(t // tm,),
            in_specs=([tile, pl.BlockSpec((1, d), lambda i: (0, 0))] if first else [tile, tile]) + weights,
            out_specs=(tile, tile) if first else tile,
            compiler_params=_params(("parallel",)),
            name="ffn",
        )(*((y, g) if first else (h, y)), w1, w3, w2)
        y, h = out if first else (out, h)
    return y


def _inproj_body(x_ref, g_ref, w_ref, gq_ref, gk_ref, cos_ref, sin_ref,
                 qt_ref, k_ref, vt_ref, u_ref, y_ref):
    h = _rms(x_ref[0], g_ref[...]).astype(BF16)
    proj = jnp.dot(h, w_ref[...], preferred_element_type=F32)
    cos = cos_ref[...]
    sin = sin_ref[...]

    def norm_rope(z, g):
        z = _rms(z, g)
        return z * cos + pltpu.roll(z, HEAD_DIM // 2, 1) * sin

    for hd in range(N_Q_HEADS):
        q = norm_rope(proj[:, hd * HEAD_DIM:(hd + 1) * HEAD_DIM], gq_ref[...]) * Q_SCALE
        qt_ref[0, hd] = q.T.astype(BF16)
    for kv in range(N_KV_HEADS):
        c0 = ATTN_W + kv * HEAD_DIM
        k_ref[0, kv] = norm_rope(proj[:, c0:c0 + HEAD_DIM], gk_ref[...]).astype(BF16)
        c0 = ATTN_W + KV_W + kv * HEAD_DIM
        vt_ref[0, kv] = proj[:, c0:c0 + HEAD_DIM].T.astype(BF16)
    c0 = ATTN_W + 2 * KV_W
    u_ref[0] = proj[:, c0:c0 + LRU_W]
    y_ref[0] = proj[:, c0 + LRU_W:c0 + 2 * LRU_W]


def _inproj(x, g, w, gq, gk, cos, sin):
    b, s, d = x.shape
    tm = _tile(s, 512)
    n = s // tm
    return pl.pallas_call(
        _inproj_body,
        out_shape=(
            jax.ShapeDtypeStruct((b, N_Q_HEADS, HEAD_DIM, s), BF16),
            jax.ShapeDtypeStruct((b, N_KV_HEADS, s, HEAD_DIM), BF16),
            jax.ShapeDtypeStruct((b, N_KV_HEADS, HEAD_DIM, s), BF16),
            jax.ShapeDtypeStruct((b, s, LRU_W), F32),
            jax.ShapeDtypeStruct((b, s, LRU_W), F32),
        ),
        grid=(b, n),
        in_specs=[
            pl.BlockSpec((1, tm, d), lambda bi, i: (bi, i, 0)),
            pl.BlockSpec((1, d), lambda bi, i: (0, 0)),
            _resident(w.shape),
            pl.BlockSpec((1, HEAD_DIM), lambda bi, i: (0, 0)),
            pl.BlockSpec((1, HEAD_DIM), lambda bi, i: (0, 0)),
            pl.BlockSpec((tm, HEAD_DIM), lambda bi, i: (i, 0)),
            pl.BlockSpec((tm, HEAD_DIM), lambda bi, i: (i, 0)),
        ],
        out_specs=(
            pl.BlockSpec((1, N_Q_HEADS, HEAD_DIM, tm), lambda bi, i: (bi, 0, 0, i)),
            pl.BlockSpec((1, N_KV_HEADS, tm, HEAD_DIM), lambda bi, i: (bi, 0, i, 0)),
            pl.BlockSpec((1, N_KV_HEADS, HEAD_DIM, tm), lambda bi, i: (bi, 0, 0, i)),
            pl.BlockSpec((1, tm, LRU_W), lambda bi, i: (bi, i, 0)),
            pl.BlockSpec((1, tm, LRU_W), lambda bi, i: (bi, i, 0)),
        ),
        compiler_params=_params(("parallel", "parallel")),
        name="inproj",
    )(x, g, w, gq, gk, cos, sin)


def _attn_body(qt_ref, k_ref, vt_ref, o_ref, kmax_ref, m_ref, l_ref, acc_ref, s0_ref, s1_ref, *, tk):
    s_len = k_ref.shape[2]
    n_kv = s_len // tk

    @pl.when(pl.program_id(2) == 0)
    def _():
        def key_chunk(j, best):
            kc = k_ref[0, 0, pl.ds(pl.multiple_of(j * tk, tk), tk), :].astype(F32)
            return jnp.maximum(best, jnp.sum(kc * kc, axis=1, keepdims=True))
        best = lax.fori_loop(0, n_kv, key_chunk, jnp.zeros((tk, 1), F32))
        kmax_ref[...] = jnp.max(best, axis=0, keepdims=True)

    for g in range(GROUP):
        qf = qt_ref[0, g].astype(F32)
        m_ref[g] = jnp.sqrt(jnp.sum(qf * qf, axis=0, keepdims=True) * kmax_ref[...])
    l_ref[...] = jnp.zeros(l_ref.shape, F32)
    acc_ref[...] = jnp.zeros(acc_ref.shape, F32)
    shift_is_safe = jnp.max(m_ref[...]) <= SAFE_SHIFT_LOG2

    def key_block(j):
        return k_ref[0, 0, pl.ds(pl.multiple_of(j * tk, tk), tk), :]

    def value_block(j):
        return vt_ref[0, 0, :, pl.ds(pl.multiple_of(j * tk, tk), tk)]

    def scores(j, s_ref):
        kb = key_block(j)
        for g in range(GROUP):
            s_ref[g] = jnp.dot(kb, qt_ref[0, g], preferred_element_type=F32)

    def accumulate(j, s_ref):
        vb = value_block(j)
        for g in range(GROUP):
            p = jnp.exp2(s_ref[g] - m_ref[g])
            l_ref[g] += jnp.sum(p, axis=0, keepdims=True)
            acc_ref[g] += jnp.dot(vb, p.astype(BF16), preferred_element_type=F32)

    def online_step(j, carry):
        kb, vb = key_block(j), value_block(j)
        for g in range(GROUP):
            s = jnp.dot(kb, qt_ref[0, g], preferred_element_type=F32)
            m_old = m_ref[g]
            m_new = jnp.maximum(m_old, jnp.max(s, axis=0, keepdims=True))
            alpha = jnp.exp2(m_old - m_new)
            p = jnp.exp2(s - m_new)
            l_ref[g] = alpha * l_ref[g] + jnp.sum(p, axis=0, keepdims=True)
            acc_ref[g] = alpha * acc_ref[g] + jnp.dot(vb, p.astype(BF16), preferred_element_type=F32)
            m_ref[g] = m_new
        return carry

    @pl.when(shift_is_safe)
    def _():
        bufs = (s0_ref, s1_ref)
        scores(0, s0_ref)

        def group(t, carry):
            for u in range(KV_UNROLL):
                j = KV_UNROLL * t + u
                scores(j + 1, bufs[(u + 1) % 2])
                accumulate(j, bufs[u % 2])
            return carry

        lax.fori_loop(0, n_kv // KV_UNROLL - 1, group, 0)
        for u in range(KV_UNROLL):
            j = n_kv - KV_UNROLL + u
            if u + 1 < KV_UNROLL:
                scores(j + 1, bufs[(u + 1) % 2])
            accumulate(j, bufs[u % 2])

    @pl.when(jnp.logical_not(shift_is_safe))
    def _():
        m_ref[...] = jnp.full(m_ref.shape, NEG_BIG, F32)
        lax.fori_loop(0, n_kv, online_step, 0)

    for g in range(GROUP):
        o = acc_ref[g] / l_ref[g]
        o_ref[0, :, g * HEAD_DIM:(g + 1) * HEAD_DIM] = o.T.astype(o_ref.dtype)


def _attn(qt, k, vt):
    b, _, _, s = qt.shape
    tq = _tile(s, 512)
    tk = _tile(s // KV_UNROLL, 512)
    assert (s // tk) % KV_UNROLL == 0
    return pl.pallas_call(
        functools.partial(_attn_body, tk=tk),
        out_shape=jax.ShapeDtypeStruct((b, s, ATTN_W), BF16),
        grid=(b, N_KV_HEADS, s // tq),
        in_specs=[
            pl.BlockSpec((1, GROUP, HEAD_DIM, tq), lambda bi, kv, i: (bi, kv, 0, i)),
            pl.BlockSpec((1, 1, s, HEAD_DIM), lambda bi, kv, i: (bi, kv, 0, 0)),
            pl.BlockSpec((1, 1, HEAD_DIM, s), lambda bi, kv, i: (bi, kv, 0, 0)),
        ],
        out_specs=pl.BlockSpec((1, tq, GROUP * HEAD_DIM), lambda bi, kv, i: (bi, i, kv)),
        scratch_shapes=[
            pltpu.VMEM((1, 1), F32),
            pltpu.VMEM((GROUP, 1, tq), F32),
            pltpu.VMEM((GROUP, 1, tq), F32),
            pltpu.VMEM((GROUP, HEAD_DIM, tq), F32),
            pltpu.VMEM((GROUP, tk, tq), F32),
            pltpu.VMEM((GROUP, tk, tq), F32),
        ],
        compiler_params=_params(("parallel", "parallel", "arbitrary")),
        name="attn",
    )(qt, k, vt)


def _one_minus_exp(y, e):
    near_zero = jnp.where(e == 1.0, -y, (1.0 - e) * y / jnp.log(e))
    return jnp.where(y < -1.0, 1.0 - e, near_zero)


def _gelu_tanh(y):
    return 0.5 * y * (1.0 + jnp.tanh(math.sqrt(2.0 / math.pi) * (y + 0.044715 * (y * y * y))))


def _sigmoid(x):
    return 0.5 * jnp.tanh(0.5 * x) + 0.5


def _seg_pitch(ts):
    return ts // SUBLANES + SUBLANES // 2


def _lru_scan(uc, wg_ref, bg_ref, nsp_ref, a_ref, b_ref, carry_ref, *, reverse):
    ts = uc.shape[0]
    seg = ts // SUBLANES
    pitch = _seg_pitch(ts)

    @pl.when(pl.program_id(1) == 0)
    def _():
        carry_ref[...] = jnp.zeros(carry_ref.shape, F32)

    ucb = uc.astype(BF16)
    for hb in range(LRU_BLOCKS):
        cols = slice(hb * LRU_BW, (hb + 1) * LRU_BW)
        gates = jnp.dot(ucb[:, cols], wg_ref[hb], preferred_element_type=F32) + bg_ref[hb]
        r = _sigmoid(gates[:, :LRU_BW])
        ig = _sigmoid(gates[:, LRU_BW:])
        log_a = nsp_ref[:, cols] * r
        a = jnp.exp(log_a)
        b = jnp.sqrt(_one_minus_exp(2.0 * log_a, a * a)) * ig * uc[:, cols]
        for sg in range(SUBLANES):
            a_ref[hb, pitch * sg:pitch * sg + seg, :] = a[seg * sg:seg * (sg + 1), :]
            b_ref[hb, pitch * sg:pitch * sg + seg, :] = b[seg * sg:seg * (sg + 1), :]

    unroll = min(SCAN_UNROLL, seg)

    def sweep(step, state):
        def trip(_, carried):
            base, st = carried
            for k in range(unroll):
                st = step(pl.ds(base + (unroll - 1 - k if reverse else k), SUBLANES, stride=pitch), st)
            return base + (-unroll if reverse else unroll), st
        first = jnp.int32(seg - unroll if reverse else 0)
        return lax.fori_loop(0, seg // unroll, trip, (first, state))[1]

    def local_step(rows, state):
        hs, ps = state
        new_h, new_p = [], []
        for hb in range(LRU_BLOCKS):
            a = a_ref[hb, rows, :]
            new_h.append(a * hs[hb] + b_ref[hb, rows, :])
            new_p.append(a * ps[hb])
        return tuple(new_h), tuple(new_p)

    zeros = tuple(jnp.zeros((SUBLANES, LRU_BW), F32) for _ in range(LRU_BLOCKS))
    ones = tuple(jnp.ones((SUBLANES, LRU_BW), F32) for _ in range(LRU_BLOCKS))
    h_end, p_end = sweep(local_step, (zeros, ones))

    row = lax.broadcasted_iota(jnp.int32, (SUBLANES, LRU_BW), 0)
    starts = []
    for hb in range(LRU_BLOCKS):
        cols = slice(hb * LRU_BW, (hb + 1) * LRU_BW)
        a, b = p_end[hb], h_end[hb]
        for dist in (1, 2, 4):
            if reverse:
                shift, valid = SUBLANES - dist, row < SUBLANES - dist
            else:
                shift, valid = dist, row >= dist
            a_s = jnp.where(valid, pltpu.roll(a, shift, 0), 1.0)
            b_s = jnp.where(valid, pltpu.roll(b, shift, 0), 0.0)
            b = a * b_s + b
            a = a * a_s
        carry = carry_ref[:, cols]
        ends = a * carry + b
        if reverse:
            starts.append(jnp.where(row == SUBLANES - 1, carry, pltpu.roll(ends, SUBLANES - 1, 0)))
            carry_ref[:, cols] = ends[0:1, :]
        else:
            starts.append(jnp.where(row == 0, carry, pltpu.roll(ends, 1, 0)))
            carry_ref[:, cols] = ends[SUBLANES - 1:SUBLANES, :]

    def final_step(rows, hs):
        new_h = []
        for hb in range(LRU_BLOCKS):
            h = a_ref[hb, rows, :] * hs[hb] + b_ref[hb, rows, :]
            b_ref[hb, rows, :] = h
            new_h.append(h)
        return tuple(new_h)

    sweep(final_step, tuple(starts))
    return jnp.concatenate(
        [jnp.concatenate([b_ref[hb, pitch * sg:pitch * sg + seg, :] for sg in range(SUBLANES)], axis=0)
         for hb in range(LRU_BLOCKS)], axis=1)


def _lru_fwd_body(u_ref, up_ref, un_ref, cw_ref, cb_ref, wg_ref, bg_ref, nsp_ref,
                  h_ref, uc_ref, ext_ref, a_ref, b_ref, carry_ref):
    ts = u_ref.shape[1]
    i = pl.program_id(1)
    ext_ref[0:SUBLANES, :] = jnp.where(i > 0, up_ref[0], 0.0)
    ext_ref[SUBLANES:SUBLANES + ts, :] = u_ref[0]
    ext_ref[SUBLANES + ts:2 * SUBLANES + ts, :] = jnp.where(i < pl.num_programs(1) - 1, un_ref[0], 0.0)
    ext = ext_ref[...]
    uc = cb_ref[...]
    for j in range(CONV_W):
        shift = (CONV_LEFT - j) % ext.shape[0]
        tap = pltpu.roll(ext, shift, 0) if shift else ext
        uc = uc + tap[SUBLANES:SUBLANES + ts, :] * cw_ref[j:j + 1, :]
    uc_ref[0] = uc
    h_ref[0] = _lru_scan(uc, wg_ref, bg_ref, nsp_ref, a_ref, b_ref, carry_ref, reverse=False)


def _lru_bwd_body(uc_ref, wg_ref, bg_ref, nsp_ref, hf_ref, y_ref, o_ref, a_ref, b_ref, carry_ref):
    h_b = _lru_scan(uc_ref[0], wg_ref, bg_ref, nsp_ref, a_ref, b_ref, carry_ref, reverse=True)
    o_ref[0] = ((hf_ref[0] + h_b) * _gelu_tanh(y_ref[0])).astype(o_ref.dtype)


def _lru(u, y, cw, cb, wg, bg, nsp):
    b, s, w = u.shape
    ts = _tile(s, 512)
    n = s // ts
    nb8 = ts // SUBLANES

    def whole(a):
        return pl.BlockSpec(a.shape, lambda bi, i: (0,) * a.ndim)

    fwd_tile = pl.BlockSpec((1, ts, w), lambda bi, i: (bi, i, 0))
    bwd_tile = pl.BlockSpec((1, ts, w), lambda bi, i: (bi, n - 1 - i, 0))
    seg_rows = SUBLANES * _seg_pitch(ts)
    scan_scratch = [pltpu.VMEM((LRU_BLOCKS, seg_rows, LRU_BW), F32), pltpu.VMEM((LRU_BLOCKS, seg_rows, LRU_BW), F32),
                    pltpu.VMEM((1, w), F32)]
    h_f, uc = pl.pallas_call(
        _lru_fwd_body,
        out_shape=(jax.ShapeDtypeStruct((b, s, w), F32), jax.ShapeDtypeStruct((b, s, w), F32)),
        grid=(b, n),
        in_specs=[
            fwd_tile,
            pl.BlockSpec((1, SUBLANES, w), lambda bi, i: (bi, jnp.maximum(i * nb8 - 1, 0), 0)),
            pl.BlockSpec((1, SUBLANES, w), lambda bi, i: (bi, jnp.minimum((i + 1) * nb8, s // SUBLANES - 1), 0)),
            whole(cw), whole(cb), whole(wg[0]), whole(bg[0]), whole(nsp[0:1]),
        ],
        out_specs=(fwd_tile, fwd_tile),
        scratch_shapes=[pltpu.VMEM((ts + 2 * SUBLANES, w), F32)] + scan_scratch,
        compiler_params=_params(("parallel", "arbitrary")),
        name="lru_fwd",
    )(u, u, u, cw, cb, wg[0], bg[0], nsp[0:1])
    return pl.pallas_call(
        _lru_bwd_body,
        out_shape=jax.ShapeDtypeStruct((b, s, w), BF16),
        grid=(b, n),
        in_specs=[bwd_tile, whole(wg[1]), whole(bg[1]), whole(nsp[1:2]), bwd_tile, bwd_tile],
        out_specs=bwd_tile,
        scratch_shapes=scan_scratch,
        compiler_params=_params(("parallel", "arbitrary")),
        name="lru_bwd",
    )(uc, wg[1], bg[1], nsp[1:2], h_f, y)


def _outproj_body(x_ref, a_ref, r_ref, wa_ref, wr_ref, o_ref):
    o_ref[...] = (x_ref[...]
                  + jnp.dot(a_ref[...], wa_ref[...], preferred_element_type=F32)
                  + jnp.dot(r_ref[...], wr_ref[...], preferred_element_type=F32))


def _outproj(x, attn, lru, wa, wr):
    t, d = x.shape
    tm = _tile(t, 512)
    return pl.pallas_call(
        _outproj_body,
        out_shape=jax.ShapeDtypeStruct((t, d), F32),
        grid=(t // tm,),
        in_specs=[
            pl.BlockSpec((tm, d), lambda i: (i, 0)),
            pl.BlockSpec((tm, attn.shape[1]), lambda i: (i, 0)),
            pl.BlockSpec((tm, lru.shape[1]), lambda i: (i, 0)),
            _resident(wa.shape),
            _resident(wr.shape),
        ],
        out_specs=pl.BlockSpec((tm, d), lambda i: (i, 0)),
        compiler_params=_params(("parallel",)),
        name="outproj",
    )(x, attn, lru, wa, wr)


def _ple_body(x_ref, p_ref, g_ref, wg_ref, wp_ref, gf_ref, o_ref, *, final):
    x = x_ref[...]
    h = _rms(x, g_ref[...]).astype(BF16)
    gate = jax.nn.sigmoid(jnp.dot(h, wg_ref[...], preferred_element_type=F32))
    emb = jnp.dot(p_ref[...].astype(BF16), wp_ref[...], preferred_element_type=F32)
    x = x + gate * emb
    o_ref[...] = _rms(x, gf_ref[...]) if final else x


def _ple(x, p, g, wg, wp, gf, *, final):
    t, d = x.shape
    tm = _tile(t, 512)
    return pl.pallas_call(
        functools.partial(_ple_body, final=final),
        out_shape=jax.ShapeDtypeStruct((t, d), F32),
        grid=(t // tm,),
        in_specs=[
            pl.BlockSpec((tm, d), lambda i: (i, 0)),
            pl.BlockSpec((tm, p.shape[1]), lambda i: (i, 0)),
            pl.BlockSpec((1, d), lambda i: (0, 0)),
            _resident(wg.shape),
            _resident(wp.shape),
            pl.BlockSpec((1, d), lambda i: (0, 0)),
        ],
        out_specs=pl.BlockSpec((tm, d), lambda i: (i, 0)),
        compiler_params=_params(("parallel",)),
        name="ple",
    )(x, p, g, wg, wp, gf)


def _rope_tables(seq_len):
    rows = seq_len // GRID_W
    inv = ROPE_THETA ** (-jnp.arange(0, AXIS_DIM, 2, dtype=F32) / AXIS_DIM)
    row_ang = jnp.arange(rows, dtype=F32)[:, None] * inv
    col_ang = jnp.arange(GRID_W, dtype=F32)[:, None] * inv

    def expand(fn):
        return jnp.concatenate([jnp.repeat(fn(row_ang), GRID_W, axis=0), jnp.tile(fn(col_ang), (rows, 1))], axis=-1)

    cos, sin = expand(jnp.cos), expand(jnp.sin)
    return jnp.concatenate([cos, cos], axis=-1), jnp.concatenate([-sin, sin], axis=-1)


def _deinterleave_perm():
    half = jnp.concatenate([jnp.arange(0, HEAD_DIM, 2), jnp.arange(1, HEAD_DIM, 2)])
    heads = jnp.arange(N_Q_HEADS + N_KV_HEADS)[:, None] * HEAD_DIM
    return (heads + half[None, :]).reshape(-1)


def kernel(x, p, norm_ffn1, w1_ffn1, w3_ffn1, w2_ffn1, norm_mix, w_in, q_norm, k_norm, conv_w, conv_b,
           lru_wa, lru_ba, lru_wi, lru_bi, lru_lambda, w_out, norm_ffn2, w1_ffn2, w3_ffn2, w2_ffn2,
           norm_ple, w_ple_gate, w_ple_proj, norm_final):
    b, s, d = x.shape
    depth = w_in.shape[0]
    t = b * s
    cos, sin = _rope_tables(s)
    perm = _deinterleave_perm()
    half = perm[:HEAD_DIM]
    qk_w = ATTN_W + KV_W

    x = x.reshape(t, d)
    for l in range(depth):
        row = lambda v: v.reshape(1, -1)
        x = _ffn(x, row(norm_ffn1[l]), w1_ffn1[l].astype(BF16), w3_ffn1[l].astype(BF16), w2_ffn1[l].astype(BF16))

        w_in_l = w_in[l].astype(BF16)
        w_in_l = jnp.concatenate([w_in_l[:, :qk_w][:, perm], w_in_l[:, qk_w:]], axis=1)
        qt, k, vt, u, y = _inproj(x.reshape(b, s, d), row(norm_mix[l]), w_in_l,
                                  row(q_norm[l][half]), row(k_norm[l][half]), cos, sin)
        attn = _attn(qt, k, vt)

        nsp = -LRU_C * jax.nn.softplus(-lru_lambda[l])
        wg = jnp.concatenate([lru_wa[l], lru_wi[l]], axis=-1).astype(BF16)
        bg = jnp.concatenate([lru_ba[l], lru_bi[l]], axis=-1)[:, :, None, :]
        lru = _lru(u, y, conv_w[l], row(conv_b[l]), wg, bg, nsp)

        w_out_l = w_out[l].astype(BF16)
        x = _outproj(x, attn.reshape(t, ATTN_W), lru.reshape(t, LRU_W), w_out_l[:ATTN_W], w_out_l[ATTN_W:])

        x = _ffn(x, row(norm_ffn2[l]), w1_ffn2[l].astype(BF16), w3_ffn2[l].astype(BF16), w2_ffn2[l].astype(BF16))

        x = _ple(x, p[l].reshape(t, -1), row(norm_ple[l]), w_ple_gate[l].astype(BF16),
                 w_ple_proj[l].astype(BF16), row(norm_final), final=(l == depth - 1))
    return x.reshape(b, s, d)
```

```python
import functools
import math

import jax
import jax.numpy as jnp
from jax import lax
from jax.experimental import pallas as pl
from jax.experimental.pallas import tpu as pltpu

F32 = jnp.float32
BF16 = jnp.bfloat16

EPS = 1e-6
HEAD_DIM = 128
N_Q_HEADS = 8
N_KV_HEADS = 2
GROUP = N_Q_HEADS // N_KV_HEADS
ATTN_W = N_Q_HEADS * HEAD_DIM
KV_W = N_KV_HEADS * HEAD_DIM
LRU_BLOCKS = 8
LRU_BW = 128
LRU_W = LRU_BLOCKS * LRU_BW
LRU_C = 8.0
CONV_W = 4
CONV_LEFT = 2
GRID_W = 64
ROPE_THETA = 10000.0
AXIS_DIM = HEAD_DIM // 2

V7X_VMEM_BYTES = 64 * 1024 * 1024
VMEM_LIMIT_BYTES = V7X_VMEM_BYTES - 8 * 1024 * 1024
SUBLANES = 8
LOG2_E = math.log2(math.e)
Q_SCALE = LOG2_E / math.sqrt(HEAD_DIM)
NEG_BIG = -1e30
SAFE_SHIFT_LOG2 = 60.0
FFN_PARTS = 2
FFN_TOKEN_TILE = 256
FFN_COL_CHUNK = 1024
TOKEN_TILE = 512
ATTN_Q_TILE = 512
ATTN_KEY_BLOCK = 512
KV_UNROLL = 8
SCAN_UNROLL = 8


def _tile(n, pref):
    t = pref
    while n % t:
        t //= 2
    return t


def _params(sem):
    return pltpu.CompilerParams(dimension_semantics=sem, vmem_limit_bytes=VMEM_LIMIT_BYTES)


def _rms(x, g):
    return x * lax.rsqrt(jnp.mean(x * x, axis=-1, keepdims=True) + EPS) * g


def _resident(shape):
    nd = len(shape)
    return pl.BlockSpec(shape, lambda *_: (0,) * nd, pipeline_mode=pl.Buffered(1))


def _ffn_part_body(*refs, first, col_chunk):
    if first:
        y_ref, g_ref, w1_ref, w3_ref, w2_ref, o_ref, h_ref = refs
        h = _rms(y_ref[...], g_ref[...]).astype(BF16)
        h_ref[...] = h
    else:
        h_ref, y_ref, w1_ref, w3_ref, w2_ref, o_ref = refs
        h = h_ref[...]
    o_ref[...] = y_ref[...]
    width = w1_ref.shape[1]
    for c0 in range(0, width, col_chunk):
        cols = slice(c0, min(c0 + col_chunk, width))
        a = jnp.dot(h, w1_ref[:, cols], preferred_element_type=F32)
        b = jnp.dot(h, w3_ref[:, cols], preferred_element_type=F32)
        z = (0.5 * a * jax.nn.sigmoid(a) * b).astype(BF16)
        o_ref[...] += jnp.dot(z, w2_ref[cols, :], preferred_element_type=F32)


def _ffn(x, g, w1, w3, w2):
    t, d = x.shape
    width = w1.shape[1] // FFN_PARTS
    tm = _tile(t, FFN_TOKEN_TILE)
    tile = pl.BlockSpec((tm, d), lambda i: (i, 0))
    y, h = x, None
    for part in range(FFN_PARTS):
        first = part == 0
        weights = [
            pl.BlockSpec((d, width), lambda i, part=part: (0, part), pipeline_mode=pl.Buffered(1)),
            pl.BlockSpec((d, width), lambda i, part=part: (0, part), pipeline_mode=pl.Buffered(1)),
            pl.BlockSpec((width, d), lambda i, part=part: (part, 0), pipeline_mode=pl.Buffered(1)),
        ]
        res = jax.ShapeDtypeStruct((t, d), F32)
        out = pl.pallas_call(
            functools.partial(_ffn_part_body, first=first, col_chunk=FFN_COL_CHUNK),
            out_shape=(res, jax.ShapeDtypeStruct((t, d), BF16)) if first else res,
            grid=(t // tm,),
            in_specs=([tile, pl.BlockSpec((1, d), lambda i: (0, 0))] if first else [tile, tile]) + weights,
            out_specs=(tile, tile) if first else tile,
            compiler_params=_params(("parallel",)),
            name="ffn",
        )(*((y, g) if first else (h, y)), w1, w3, w2)
        y, h = out if first else (out, h)
    return y


def _inproj_body(x_ref, g_ref, w_ref, gq_ref, gk_ref, cos_ref, sin_ref,
                 qt_ref, k_ref, vt_ref, u_ref, y_ref):
    h = _rms(x_ref[0], g_ref[...]).astype(BF16)
    proj = jnp.dot(h, w_ref[...], preferred_element_type=F32)
    cos = cos_ref[...]
    sin = sin_ref[...]

    def norm_rope(z, g):
        z = _rms(z, g)
        return z * cos + pltpu.roll(z, HEAD_DIM // 2, 1) * sin

    for hd in range(N_Q_HEADS):
        q = norm_rope(proj[:, hd * HEAD_DIM:(hd + 1) * HEAD_DIM], gq_ref[...]) * Q_SCALE
        qt_ref[0, hd] = q.T.astype(BF16)
    for kv in range(N_KV_HEADS):
        c0 = ATTN_W + kv * HEAD_DIM
        k_ref[0, kv] = norm_rope(proj[:, c0:c0 + HEAD_DIM], gk_ref[...]).astype(BF16)
        c0 = ATTN_W + KV_W + kv * HEAD_DIM
        vt_ref[0, kv] = proj[:, c0:c0 + HEAD_DIM].T.astype(BF16)
    c0 = ATTN_W + 2 * KV_W
    u_ref[0] = proj[:, c0:c0 + LRU_W]
    y_ref[0] = proj[:, c0 + LRU_W:c0 + 2 * LRU_W]


def _inproj(x, g, w, gq, gk, cos, sin):
    b, s, d = x.shape
    tm = _tile(s, TOKEN_TILE)
    n = s // tm
    return pl.pallas_call(
        _inproj_body,
        out_shape=(
            jax.ShapeDtypeStruct((b, N_Q_HEADS, HEAD_DIM, s), BF16),
            jax.ShapeDtypeStruct((b, N_KV_HEADS, s, HEAD_DIM), BF16),
            jax.ShapeDtypeStruct((b, N_KV_HEADS, HEAD_DIM, s), BF16),
            jax.ShapeDtypeStruct((b, s, LRU_W), F32),
            jax.ShapeDtypeStruct((b, s, LRU_W), F32),
        ),
        grid=(b, n),
        in_specs=[
            pl.BlockSpec((1, tm, d), lambda bi, i: (bi, i, 0)),
            pl.BlockSpec((1, d), lambda bi, i: (0, 0)),
            _resident(w.shape),
            pl.BlockSpec((1, HEAD_DIM), lambda bi, i: (0, 0)),
            pl.BlockSpec((1, HEAD_DIM), lambda bi, i: (0, 0)),
            pl.BlockSpec((tm, HEAD_DIM), lambda bi, i: (i, 0)),
            pl.BlockSpec((tm, HEAD_DIM), lambda bi, i: (i, 0)),
        ],
        out_specs=(
            pl.BlockSpec((1, N_Q_HEADS, HEAD_DIM, tm), lambda bi, i: (bi, 0, 0, i)),
            pl.BlockSpec((1, N_KV_HEADS, tm, HEAD_DIM), lambda bi, i: (bi, 0, i, 0)),
            pl.BlockSpec((1, N_KV_HEADS, HEAD_DIM, tm), lambda bi, i: (bi, 0, 0, i)),
            pl.BlockSpec((1, tm, LRU_W), lambda bi, i: (bi, i, 0)),
            pl.BlockSpec((1, tm, LRU_W), lambda bi, i: (bi, i, 0)),
        ),
        compiler_params=_params(("parallel", "parallel")),
        name="inproj",
    )(x, g, w, gq, gk, cos, sin)


def _attn_body(qt_ref, k_ref, vt_ref, o_ref, kmax_ref, m_ref, l_ref, acc_ref, s0_ref, s1_ref, *, tk):
    s_len = k_ref.shape[2]
    n_kv = s_len // tk

    @pl.when(pl.program_id(2) == 0)
    def _():
        def key_chunk(j, best):
            kc = k_ref[0, 0, pl.ds(pl.multiple_of(j * tk, tk), tk), :].astype(F32)
            return jnp.maximum(best, jnp.sum(kc * kc, axis=1, keepdims=True))
        best = lax.fori_loop(0, n_kv, key_chunk, jnp.zeros((tk, 1), F32))
        kmax_ref[...] = jnp.max(best, axis=0, keepdims=True)

    for g in range(GROUP):
        qf = qt_ref[0, g].astype(F32)
        m_ref[g] = jnp.sqrt(jnp.sum(qf * qf, axis=0, keepdims=True) * kmax_ref[...])
    l_ref[...] = jnp.zeros(l_ref.shape, F32)
    acc_ref[...] = jnp.zeros(acc_ref.shape, F32)
    shift_is_safe = jnp.max(m_ref[...]) <= SAFE_SHIFT_LOG2

    def key_block(j):
        return k_ref[0, 0, pl.ds(pl.multiple_of(j * tk, tk), tk), :]

    def value_block(j):
        return vt_ref[0, 0, :, pl.ds(pl.multiple_of(j * tk, tk), tk)]

    def scores(j, s_ref):
        kb = key_block(j)
        for g in range(GROUP):
            s_ref[g] = jnp.dot(kb, qt_ref[0, g], preferred_element_type=F32)

    def accumulate(j, s_ref):
        vb = value_block(j)
        for g in range(GROUP):
            p = jnp.exp2(s_ref[g] - m_ref[g])
            l_ref[g] += jnp.sum(p, axis=0, keepdims=True)
            acc_ref[g] += jnp.dot(vb, p.astype(BF16), preferred_element_type=F32)

    def online_step(j, carry):
        kb, vb = key_block(j), value_block(j)
        for g in range(GROUP):
            s = jnp.dot(kb, qt_ref[0, g], preferred_element_type=F32)
            m_old = m_ref[g]
            m_new = jnp.maximum(m_old, jnp.max(s, axis=0, keepdims=True))
            alpha = jnp.exp2(m_old - m_new)
            p = jnp.exp2(s - m_new)
            l_ref[g] = alpha * l_ref[g] + jnp.sum(p, axis=0, keepdims=True)
            acc_ref[g] = alpha * acc_ref[g] + jnp.dot(vb, p.astype(BF16), preferred_element_type=F32)
            m_ref[g] = m_new
        return carry

    @pl.when(shift_is_safe)
    def _():
        bufs = (s0_ref, s1_ref)
        scores(0, s0_ref)

        def group(t, carry):
            for u in range(KV_UNROLL):
                j = KV_UNROLL * t + u
                scores(j + 1, bufs[(u + 1) % 2])
                accumulate(j, bufs[u % 2])
            return carry

        lax.fori_loop(0, n_kv // KV_UNROLL - 1, group, 0)
        for u in range(KV_UNROLL):
            j = n_kv - KV_UNROLL + u
            if u + 1 < KV_UNROLL:
                scores(j + 1, bufs[(u + 1) % 2])
            accumulate(j, bufs[u % 2])

    @pl.when(jnp.logical_not(shift_is_safe))
    def _():
        m_ref[...] = jnp.full(m_ref.shape, NEG_BIG, F32)
        lax.fori_loop(0, n_kv, online_step, 0)

    for g in range(GROUP):
        o = acc_ref[g] / l_ref[g]
        o_ref[0, :, g * HEAD_DIM:(g + 1) * HEAD_DIM] = o.T.astype(o_ref.dtype)


def _attn(qt, k, vt):
    b, _, _, s = qt.shape
    tq = _tile(s, ATTN_Q_TILE)
    tk = _tile(s // KV_UNROLL, ATTN_KEY_BLOCK)
    assert (s // tk) % KV_UNROLL == 0
    return pl.pallas_call(
        functools.partial(_attn_body, tk=tk),
        out_shape=jax.ShapeDtypeStruct((b, s, ATTN_W), BF16),
        grid=(b, N_KV_HEADS, s // tq),
        in_specs=[
            pl.BlockSpec((1, GROUP, HEAD_DIM, tq), lambda bi, kv, i: (bi, kv, 0, i)),
            pl.BlockSpec((1, 1, s, HEAD_DIM), lambda bi, kv, i: (bi, kv, 0, 0)),
            pl.BlockSpec((1, 1, HEAD_DIM, s), lambda bi, kv, i: (bi, kv, 0, 0)),
        ],
        out_specs=pl.BlockSpec((1, tq, GROUP * HEAD_DIM), lambda bi, kv, i: (bi, i, kv)),
        scratch_shapes=[
            pltpu.VMEM((1, 1), F32),
            pltpu.VMEM((GROUP, 1, tq), F32),
            pltpu.VMEM((GROUP, 1, tq), F32),
            pltpu.VMEM((GROUP, HEAD_DIM, tq), F32),
            pltpu.VMEM((GROUP, tk, tq), F32),
            pltpu.VMEM((GROUP, tk, tq), F32),
        ],
        compiler_params=_params(("parallel", "parallel", "arbitrary")),
        name="attn",
    )(qt, k, vt)


def _gelu_tanh(y):
    return 0.5 * y * (1.0 + jnp.tanh(math.sqrt(2.0 / math.pi) * (y + 0.044715 * (y * y * y))))


def _seg_pitch(ts):
    return ts // SUBLANES + SUBLANES // 2


def _lru_scan(uc, wg_ref, bg_ref, hc_ref, a_ref, b_ref, carry_ref, *, reverse):
    ts = uc.shape[0]
    seg = ts // SUBLANES
    pitch = _seg_pitch(ts)

    @pl.when(pl.program_id(1) == 0)
    def _():
        carry_ref[...] = jnp.zeros(carry_ref.shape, F32)

    ucb = uc.astype(BF16)
    for hb in range(LRU_BLOCKS):
        cols = slice(hb * LRU_BW, (hb + 1) * LRU_BW)
        half = jnp.dot(ucb[:, cols], wg_ref[hb], preferred_element_type=F32) + bg_ref[hb]
        t_r = jnp.tanh(half[:, :LRU_BW])
        t_i = jnp.tanh(half[:, LRU_BW:])
        n = hc_ref[:, cols] * t_r + hc_ref[:, cols]
        a = jnp.exp2(n * (-LOG2_E))
        b = jnp.sqrt(jnp.tanh(n) * (1.0 + a * a)) * (0.5 * t_i + 0.5) * uc[:, cols]
        for sg in range(SUBLANES):
            a_ref[hb, pitch * sg:pitch * sg + seg, :] = a[seg * sg:seg * (sg + 1), :]
            b_ref[hb, pitch * sg:pitch * sg + seg, :] = b[seg * sg:seg * (sg + 1), :]

    unroll = min(SCAN_UNROLL, seg)

    def sweep(step, state):
        def trip(_, carried):
            base, st = carried
            for k in range(unroll):
                st = step(pl.ds(base + (unroll - 1 - k if reverse else k), SUBLANES, stride=pitch), st)
            return base + (-unroll if reverse else unroll), st
        first = jnp.int32(seg - unroll if reverse else 0)
        return lax.fori_loop(0, seg // unroll, trip, (first, state))[1]

    def local_step(rows, state):
        hs, ps = state
        new_h, new_p = [], []
        for hb in range(LRU_BLOCKS):
            a = a_ref[hb, rows, :]
            new_h.append(a * hs[hb] + b_ref[hb, rows, :])
            new_p.append(a * ps[hb])
        return tuple(new_h), tuple(new_p)

    zeros = tuple(jnp.zeros((SUBLANES, LRU_BW), F32) for _ in range(LRU_BLOCKS))
    ones = tuple(jnp.ones((SUBLANES, LRU_BW), F32) for _ in range(LRU_BLOCKS))
    h_end, p_end = sweep(local_step, (zeros, ones))

    row = lax.broadcasted_iota(jnp.int32, (SUBLANES, LRU_BW), 0)
    starts = []
    for hb in range(LRU_BLOCKS):
        cols = slice(hb * LRU_BW, (hb + 1) * LRU_BW)
        a, b = p_end[hb], h_end[hb]
        for dist in (1, 2, 4):
            if reverse:
                shift, valid = SUBLANES - dist, row < SUBLANES - dist
            else:
                shift, valid = dist, row >= dist
            a_s = jnp.where(valid, pltpu.roll(a, shift, 0), 1.0)
            b_s = jnp.where(valid, pltpu.roll(b, shift, 0), 0.0)
            b = a * b_s + b
            a = a * a_s
        carry = carry_ref[:, cols]
        ends = a * carry + b
        if reverse:
            starts.append(jnp.where(row == SUBLANES - 1, carry, pltpu.roll(ends, SUBLANES - 1, 0)))
            carry_ref[:, cols] = ends[0:1, :]
        else:
            starts.append(jnp.where(row == 0, carry, pltpu.roll(ends, 1, 0)))
            carry_ref[:, cols] = ends[SUBLANES - 1:SUBLANES, :]

    def final_step(rows, hs):
        new_h = []
        for hb in range(LRU_BLOCKS):
            h = a_ref[hb, rows, :] * hs[hb] + b_ref[hb, rows, :]
            b_ref[hb, rows, :] = h
            new_h.append(h)
        return tuple(new_h)

    sweep(final_step, tuple(starts))
    return jnp.concatenate(
        [jnp.concatenate([b_ref[hb, pitch * sg:pitch * sg + seg, :] for sg in range(SUBLANES)], axis=0)
         for hb in range(LRU_BLOCKS)], axis=1)


def _lru_fwd_body(u_ref, up_ref, un_ref, cw_ref, cb_ref, wg_ref, bg_ref, hc_ref,
                  h_ref, uc_ref, ext_ref, a_ref, b_ref, carry_ref):
    ts = u_ref.shape[1]
    i = pl.program_id(1)
    ext_ref[0:SUBLANES, :] = jnp.where(i > 0, up_ref[0], 0.0)
    ext_ref[SUBLANES:SUBLANES + ts, :] = u_ref[0]
    ext_ref[SUBLANES + ts:2 * SUBLANES + ts, :] = jnp.where(i < pl.num_programs(1) - 1, un_ref[0], 0.0)
    ext = ext_ref[...]
    uc = cb_ref[...]
    for j in range(CONV_W):
        shift = (CONV_LEFT - j) % ext.shape[0]
        tap = pltpu.roll(ext, shift, 0) if shift else ext
        uc = uc + tap[SUBLANES:SUBLANES + ts, :] * cw_ref[j:j + 1, :]
    uc_ref[0] = uc
    h_ref[0] = _lru_scan(uc, wg_ref, bg_ref, hc_ref, a_ref, b_ref, carry_ref, reverse=False)


def _lru_bwd_body(uc_ref, wg_ref, bg_ref, hc_ref, hf_ref, y_ref, o_ref, a_ref, b_ref, carry_ref):
    h_b = _lru_scan(uc_ref[0], wg_ref, bg_ref, hc_ref, a_ref, b_ref, carry_ref, reverse=True)
    o_ref[0] = ((hf_ref[0] + h_b) * _gelu_tanh(y_ref[0])).astype(o_ref.dtype)


def _lru(u, y, cw, cb, wg, bg, hc):
    b, s, w = u.shape
    ts = _tile(s, TOKEN_TILE)
    n = s // ts
    nb8 = ts // SUBLANES

    def whole(a):
        return pl.BlockSpec(a.shape, lambda bi, i: (0,) * a.ndim)

    fwd_tile = pl.BlockSpec((1, ts, w), lambda bi, i: (bi, i, 0))
    bwd_tile = pl.BlockSpec((1, ts, w), lambda bi, i: (bi, n - 1 - i, 0))
    seg_rows = SUBLANES * _seg_pitch(ts)
    scan_scratch = [pltpu.VMEM((LRU_BLOCKS, seg_rows, LRU_BW), F32), pltpu.VMEM((LRU_BLOCKS, seg_rows, LRU_BW), F32),
                    pltpu.VMEM((1, w), F32)]
    h_f, uc = pl.pallas_call(
        _lru_fwd_body,
        out_shape=(jax.ShapeDtypeStruct((b, s, w), F32), jax.ShapeDtypeStruct((b, s, w), F32)),
        grid=(b, n),
        in_specs=[
            fwd_tile,
            pl.BlockSpec((1, SUBLANES, w), lambda bi, i: (bi, jnp.maximum(i * nb8 - 1, 0), 0)),
            pl.BlockSpec((1, SUBLANES, w), lambda bi, i: (bi, jnp.minimum((i + 1) * nb8, s // SUBLANES - 1), 0)),
            whole(cw), whole(cb), whole(wg[0]), whole(bg[0]), whole(hc[0:1]),
        ],
        out_specs=(fwd_tile, fwd_tile),
        scratch_shapes=[pltpu.VMEM((ts + 2 * SUBLANES, w), F32)] + scan_scratch,
        compiler_params=_params(("parallel", "arbitrary")),
        name="lru_fwd",
    )(u, u, u, cw, cb, wg[0], bg[0], hc[0:1])
    return pl.pallas_call(
        _lru_bwd_body,
        out_shape=jax.ShapeDtypeStruct((b, s, w), BF16),
        grid=(b, n),
        in_specs=[bwd_tile, whole(wg[1]), whole(bg[1]), whole(hc[1:2]), bwd_tile, bwd_tile],
        out_specs=bwd_tile,
        scratch_shapes=scan_scratch,
        compiler_params=_params(("parallel", "arbitrary")),
        name="lru_bwd",
    )(uc, wg[1], bg[1], hc[1:2], h_f, y)


def _outproj_body(x_ref, a_ref, r_ref, wa_ref, wr_ref, o_ref):
    o_ref[...] = (x_ref[...]
                  + jnp.dot(a_ref[...], wa_ref[...], preferred_element_type=F32)
                  + jnp.dot(r_ref[...], wr_ref[...], preferred_element_type=F32))


def _outproj(x, attn, lru, wa, wr):
    t, d = x.shape
    tm = _tile(t, TOKEN_TILE)
    return pl.pallas_call(
        _outproj_body,
        out_shape=jax.ShapeDtypeStruct((t, d), F32),
        grid=(t // tm,),
        in_specs=[
            pl.BlockSpec((tm, d), lambda i: (i, 0)),
            pl.BlockSpec((tm, attn.shape[1]), lambda i: (i, 0)),
            pl.BlockSpec((tm, lru.shape[1]), lambda i: (i, 0)),
            _resident(wa.shape),
            _resident(wr.shape),
        ],
        out_specs=pl.BlockSpec((tm, d), lambda i: (i, 0)),
        compiler_params=_params(("parallel",)),
        name="outproj",
    )(x, attn, lru, wa, wr)


def _ple_body(x_ref, p_ref, g_ref, wg_ref, wp_ref, gf_ref, o_ref, *, final):
    x = x_ref[...]
    h = _rms(x, g_ref[...]).astype(BF16)
    gate = jax.nn.sigmoid(jnp.dot(h, wg_ref[...], preferred_element_type=F32))
    emb = jnp.dot(p_ref[...].astype(BF16), wp_ref[...], preferred_element_type=F32)
    x = x + gate * emb
    o_ref[...] = _rms(x, gf_ref[...]) if final else x


def _ple(x, p, g, wg, wp, gf, *, final):
    t, d = x.shape
    tm = _tile(t, TOKEN_TILE)
    return pl.pallas_call(
        functools.partial(_ple_body, final=final),
        out_shape=jax.ShapeDtypeStruct((t, d), F32),
        grid=(t // tm,),
        in_specs=[
            pl.BlockSpec((tm, d), lambda i: (i, 0)),
            pl.BlockSpec((tm, p.shape[1]), lambda i: (i, 0)),
            pl.BlockSpec((1, d), lambda i: (0, 0)),
            _resident(wg.shape),
            _resident(wp.shape),
            pl.BlockSpec((1, d), lambda i: (0, 0)),
        ],
        out_specs=pl.BlockSpec((tm, d), lambda i: (i, 0)),
        compiler_params=_params(("parallel",)),
        name="ple",
    )(x, p, g, wg, wp, gf)


def _rope_tables(seq_len):
    rows = seq_len // GRID_W
    inv = ROPE_THETA ** (-jnp.arange(0, AXIS_DIM, 2, dtype=F32) / AXIS_DIM)
    row_ang = jnp.arange(rows, dtype=F32)[:, None] * inv
    col_ang = jnp.arange(GRID_W, dtype=F32)[:, None] * inv

    def expand(fn):
        return jnp.concatenate([jnp.repeat(fn(row_ang), GRID_W, axis=0), jnp.tile(fn(col_ang), (rows, 1))], axis=-1)

    cos, sin = expand(jnp.cos), expand(jnp.sin)
    return jnp.concatenate([cos, cos], axis=-1), jnp.concatenate([-sin, sin], axis=-1)


def _deinterleave_perm():
    half = jnp.concatenate([jnp.arange(0, HEAD_DIM, 2), jnp.arange(1, HEAD_DIM, 2)])
    heads = jnp.arange(N_Q_HEADS + N_KV_HEADS)[:, None] * HEAD_DIM
    return (heads + half[None, :]).reshape(-1)


def kernel(x, p, norm_ffn1, w1_ffn1, w3_ffn1, w2_ffn1, norm_mix, w_in, q_norm, k_norm, conv_w, conv_b,
           lru_wa, lru_ba, lru_wi, lru_bi, lru_lambda, w_out, norm_ffn2, w1_ffn2, w3_ffn2, w2_ffn2,
           norm_ple, w_ple_gate, w_ple_proj, norm_final):
    b, s, d = x.shape
    depth = w_in.shape[0]
    t = b * s
    cos, sin = _rope_tables(s)
    perm = _deinterleave_perm()
    half = perm[:HEAD_DIM]
    qk_w = ATTN_W + KV_W

    x = x.reshape(t, d)
    for l in range(depth):
        row = lambda v: v.reshape(1, -1)
        x = _ffn(x, row(norm_ffn1[l]), w1_ffn1[l].astype(BF16), w3_ffn1[l].astype(BF16), w2_ffn1[l].astype(BF16))

        w_in_l = w_in[l].astype(BF16)
        w_in_l = jnp.concatenate([w_in_l[:, :qk_w][:, perm], w_in_l[:, qk_w:]], axis=1)
        qt, k, vt, u, y = _inproj(x.reshape(b, s, d), row(norm_mix[l]), w_in_l,
                                  row(q_norm[l][half]), row(k_norm[l][half]), cos, sin)
        attn = _attn(qt, k, vt)

        hc = (0.5 * LRU_C) * jax.nn.softplus(-lru_lambda[l])
        wg = (0.5 * jnp.concatenate([lru_wa[l], lru_wi[l]], axis=-1)).astype(BF16)
        bg = 0.5 * jnp.concatenate([lru_ba[l], lru_bi[l]], axis=-1)[:, :, None, :]
        lru = _lru(u, y, conv_w[l], row(conv_b[l]), wg, bg, hc)

        w_out_l = w_out[l].astype(BF16)
        x = _outproj(x, attn.reshape(t, ATTN_W), lru.reshape(t, LRU_W), w_out_l[:ATTN_W], w_out_l[ATTN_W:])

        x = _ffn(x, row(norm_ffn2[l]), w1_ffn2[l].astype(BF16), w3_ffn2[l].astype(BF16), w2_ffn2[l].astype(BF16))

        x = _ple(x, p[l].reshape(t, -1), row(norm_ple[l]), w_ple_gate[l].astype(BF16),
                 w_ple_proj[l].astype(BF16), row(norm_final), final=(l == depth - 1))
    return x.reshape(b, s, d)
```

```python
import functools
import math

import jax
import jax.numpy as jnp
from jax import lax
from jax.experimental import pallas as pl
from jax.experimental.pallas import tpu as pltpu

F32 = jnp.float32
BF16 = jnp.bfloat16

EPS = 1e-6
HEAD_DIM = 128
N_Q_HEADS = 8
N_KV_HEADS = 2
GROUP = N_Q_HEADS // N_KV_HEADS
ATTN_W = N_Q_HEADS * HEAD_DIM
KV_W = N_KV_HEADS * HEAD_DIM
LRU_BLOCKS = 8
LRU_BW = 128
LRU_W = LRU_BLOCKS * LRU_BW
LRU_C = 8.0
CONV_W = 4
CONV_LEFT = 2
GRID_W = 64
ROPE_THETA = 10000.0
AXIS_DIM = HEAD_DIM // 2

V7X_VMEM_BYTES = 64 * 1024 * 1024
VMEM_LIMIT_BYTES = V7X_VMEM_BYTES - 4 * 1024 * 1024
SUBLANES = 8
LOG2_E = math.log2(math.e)
Q_SCALE = LOG2_E / math.sqrt(HEAD_DIM)
NEG_BIG = -1e30
SAFE_SHIFT_LOG2 = 60.0
FFN_PARTS = 2
FFN_TOKEN_TILE = 512
FFN_COL_CHUNK = 1024
TOKEN_TILE = 512
ATTN_Q_TILE = 512
ATTN_KEY_BLOCK = 512
KV_UNROLL = 8
SCAN_UNROLL = 8


def _tile(n, pref):
    t = pref
    while n % t:
        t //= 2
    return t


def _params(sem):
    return pltpu.CompilerParams(dimension_semantics=sem, vmem_limit_bytes=VMEM_LIMIT_BYTES)


def _rms(x, g):
    return x * lax.rsqrt(jnp.mean(x * x, axis=-1, keepdims=True) + EPS) * g


def _resident(shape):
    nd = len(shape)
    return pl.BlockSpec(shape, lambda *_: (0,) * nd, pipeline_mode=pl.Buffered(1))


def _ffn_part_body(*refs, first, col_chunk):
    if first:
        y_ref, g_ref, w1_ref, w3_ref, w2_ref, o_ref, h_ref = refs
        h = _rms(y_ref[...], g_ref[...]).astype(BF16)
        h_ref[...] = h
    else:
        h_ref, y_ref, w1_ref, w3_ref, w2_ref, o_ref = refs
        h = h_ref[...]
    o_ref[...] = y_ref[...]
    width = w1_ref.shape[1]
    for c0 in range(0, width, col_chunk):
        cols = slice(c0, min(c0 + col_chunk, width))
        a = jnp.dot(h, w1_ref[:, cols], preferred_element_type=F32)
        b = jnp.dot(h, w3_ref[:, cols], preferred_element_type=F32)
        z = (0.5 * a * jax.nn.sigmoid(a) * b).astype(BF16)
        o_ref[...] += jnp.dot(z, w2_ref[cols, :], preferred_element_type=F32)


def _ffn(x, g, w1, w3, w2):
    t, d = x.shape
    width = w1.shape[1] // FFN_PARTS
    tm = _tile(t, FFN_TOKEN_TILE)
    tile = pl.BlockSpec((tm, d), lambda i: (i, 0))
    y, h = x, None
    for part in range(FFN_PARTS):
        first = part == 0
        weights = [
            pl.BlockSpec((d, width), lambda i, part=part: (0, part), pipeline_mode=pl.Buffered(1)),
            pl.BlockSpec((d, width), lambda i, part=part: (0, part), pipeline_mode=pl.Buffered(1)),
            pl.BlockSpec((width, d), lambda i, part=part: (part, 0), pipeline_mode=pl.Buffered(1)),
        ]
        res = jax.ShapeDtypeStruct((t, d), F32)
        out = pl.pallas_call(
            functools.partial(_ffn_part_body, first=first, col_chunk=FFN_COL_CHUNK),
            out_shape=(res, jax.ShapeDtypeStruct((t, d), BF16)) if first else res,
            grid=(t // tm,),
            in_specs=([tile, pl.BlockSpec((1, d), lambda i: (0, 0))] if first else [tile, tile]) + weights,
            out_specs=(tile, tile) if first else tile,
            compiler_params=_params(("parallel",)),
            name="ffn",
        )(*((y, g) if first else (h, y)), w1, w3, w2)
        y, h = out if first else (out, h)
    return y


def _inproj_body(x_ref, g_ref, w_ref, gq_ref, gk_ref, cos_ref, sin_ref,
                 qt_ref, k_ref, vt_ref, u_ref, y_ref):
    h = _rms(x_ref[0], g_ref[...]).astype(BF16)
    proj = jnp.dot(h, w_ref[...], preferred_element_type=F32)
    cos = cos_ref[...]
    sin = sin_ref[...]

    def norm_rope(z, g):
        z = _rms(z, g)
        return z * cos + pltpu.roll(z, HEAD_DIM // 2, 1) * sin

    for hd in range(N_Q_HEADS):
        q = norm_rope(proj[:, hd * HEAD_DIM:(hd + 1) * HEAD_DIM], gq_ref[...]) * Q_SCALE
        qt_ref[0, hd] = q.T.astype(BF16)
    for kv in range(N_KV_HEADS):
        c0 = ATTN_W + kv * HEAD_DIM
        k_ref[0, kv] = norm_rope(proj[:, c0:c0 + HEAD_DIM], gk_ref[...]).astype(BF16)
        c0 = ATTN_W + KV_W + kv * HEAD_DIM
        vt_ref[0, kv] = proj[:, c0:c0 + HEAD_DIM].T.astype(BF16)
    c0 = ATTN_W + 2 * KV_W
    u_ref[0] = proj[:, c0:c0 + LRU_W]
    y_ref[0] = proj[:, c0 + LRU_W:c0 + 2 * LRU_W]


def _inproj(x, g, w, gq, gk, cos, sin):
    b, s, d = x.shape
    tm = _tile(s, TOKEN_TILE)
    n = s // tm
    return pl.pallas_call(
        _inproj_body,
        out_shape=(
            jax.ShapeDtypeStruct((b, N_Q_HEADS, HEAD_DIM, s), BF16),
            jax.ShapeDtypeStruct((b, N_KV_HEADS, s, HEAD_DIM), BF16),
            jax.ShapeDtypeStruct((b, N_KV_HEADS, HEAD_DIM, s), BF16),
            jax.ShapeDtypeStruct((b, s, LRU_W), F32),
            jax.ShapeDtypeStruct((b, s, LRU_W), F32),
        ),
        grid=(b, n),
        in_specs=[
            pl.BlockSpec((1, tm, d), lambda bi, i: (bi, i, 0)),
            pl.BlockSpec((1, d), lambda bi, i: (0, 0)),
            _resident(w.shape),
            pl.BlockSpec((1, HEAD_DIM), lambda bi, i: (0, 0)),
            pl.BlockSpec((1, HEAD_DIM), lambda bi, i: (0, 0)),
            pl.BlockSpec((tm, HEAD_DIM), lambda bi, i: (i, 0)),
            pl.BlockSpec((tm, HEAD_DIM), lambda bi, i: (i, 0)),
        ],
        out_specs=(
            pl.BlockSpec((1, N_Q_HEADS, HEAD_DIM, tm), lambda bi, i: (bi, 0, 0, i)),
            pl.BlockSpec((1, N_KV_HEADS, tm, HEAD_DIM), lambda bi, i: (bi, 0, i, 0)),
            pl.BlockSpec((1, N_KV_HEADS, HEAD_DIM, tm), lambda bi, i: (bi, 0, 0, i)),
            pl.BlockSpec((1, tm, LRU_W), lambda bi, i: (bi, i, 0)),
            pl.BlockSpec((1, tm, LRU_W), lambda bi, i: (bi, i, 0)),
        ),
        compiler_params=_params(("parallel", "parallel")),
        name="inproj",
    )(x, g, w, gq, gk, cos, sin)


def _attn_body(qt_ref, k_ref, vt_ref, o_ref, kmax_ref, m_ref, l_ref, acc_ref, s0_ref, s1_ref, *, tk):
    s_len = k_ref.shape[2]
    n_kv = s_len // tk

    @pl.when(pl.program_id(2) == 0)
    def _():
        def key_chunk(j, best):
            kc = k_ref[0, 0, pl.ds(pl.multiple_of(j * tk, tk), tk), :].astype(F32)
            return jnp.maximum(best, jnp.sum(kc * kc, axis=1, keepdims=True))
        best = lax.fori_loop(0, n_kv, key_chunk, jnp.zeros((tk, 1), F32))
        kmax_ref[...] = jnp.max(best, axis=0, keepdims=True)

    for g in range(GROUP):
        qf = qt_ref[0, g].astype(F32)
        m_ref[g] = jnp.sqrt(jnp.sum(qf * qf, axis=0, keepdims=True) * kmax_ref[...])
    l_ref[...] = jnp.zeros(l_ref.shape, F32)
    acc_ref[...] = jnp.zeros(acc_ref.shape, F32)
    shift_is_safe = jnp.max(m_ref[...]) <= SAFE_SHIFT_LOG2

    def key_block(j):
        return k_ref[0, 0, pl.ds(pl.multiple_of(j * tk, tk), tk), :]

    def value_block(j):
        return vt_ref[0, 0, :, pl.ds(pl.multiple_of(j * tk, tk), tk)]

    def scores(j, s_ref):
        kb = key_block(j)
        for g in range(GROUP):
            s_ref[g] = jnp.dot(kb, qt_ref[0, g], preferred_element_type=F32)

    def accumulate(j, s_ref):
        vb = value_block(j)
        for g in range(GROUP):
            p = jnp.exp2(s_ref[g] - m_ref[g])
            l_ref[g] += jnp.sum(p, axis=0, keepdims=True)
            acc_ref[g] += jnp.dot(vb, p.astype(BF16), preferred_element_type=F32)

    def online_step(j, carry):
        kb, vb = key_block(j), value_block(j)
        for g in range(GROUP):
            s = jnp.dot(kb, qt_ref[0, g], preferred_element_type=F32)
            m_old = m_ref[g]
            m_new = jnp.maximum(m_old, jnp.max(s, axis=0, keepdims=True))
            alpha = jnp.exp2(m_old - m_new)
            p = jnp.exp2(s - m_new)
            l_ref[g] = alpha * l_ref[g] + jnp.sum(p, axis=0, keepdims=True)
            acc_ref[g] = alpha * acc_ref[g] + jnp.dot(vb, p.astype(BF16), preferred_element_type=F32)
            m_ref[g] = m_new
        return carry

    @pl.when(shift_is_safe)
    def _():
        bufs = (s0_ref, s1_ref)
        scores(0, s0_ref)

        def group(t, carry):
            for u in range(KV_UNROLL):
                j = KV_UNROLL * t + u
                scores(j + 1, bufs[(u + 1) % 2])
                accumulate(j, bufs[u % 2])
            return carry

        lax.fori_loop(0, n_kv // KV_UNROLL - 1, group, 0)
        for u in range(KV_UNROLL):
            j = n_kv - KV_UNROLL + u
            if u + 1 < KV_UNROLL:
                scores(j + 1, bufs[(u + 1) % 2])
            accumulate(j, bufs[u % 2])

    @pl.when(jnp.logical_not(shift_is_safe))
    def _():
        m_ref[...] = jnp.full(m_ref.shape, NEG_BIG, F32)
        lax.fori_loop(0, n_kv, online_step, 0)

    for g in range(GROUP):
        o = acc_ref[g] / l_ref[g]
        o_ref[0, :, g * HEAD_DIM:(g + 1) * HEAD_DIM] = o.T.astype(o_ref.dtype)


def _attn(qt, k, vt):
    b, _, _, s = qt.shape
    tq = _tile(s, ATTN_Q_TILE)
    tk = _tile(s // KV_UNROLL, ATTN_KEY_BLOCK)
    assert (s // tk) % KV_UNROLL == 0
    return pl.pallas_call(
        functools.partial(_attn_body, tk=tk),
        out_shape=jax.ShapeDtypeStruct((b, s, ATTN_W), BF16),
        grid=(b, N_KV_HEADS, s // tq),
        in_specs=[
            pl.BlockSpec((1, GROUP, HEAD_DIM, tq), lambda bi, kv, i: (bi, kv, 0, i)),
            pl.BlockSpec((1, 1, s, HEAD_DIM), lambda bi, kv, i: (bi, kv, 0, 0)),
            pl.BlockSpec((1, 1, HEAD_DIM, s), lambda bi, kv, i: (bi, kv, 0, 0)),
        ],
        out_specs=pl.BlockSpec((1, tq, GROUP * HEAD_DIM), lambda bi, kv, i: (bi, i, kv)),
        scratch_shapes=[
            pltpu.VMEM((1, 1), F32),
            pltpu.VMEM((GROUP, 1, tq), F32),
            pltpu.VMEM((GROUP, 1, tq), F32),
            pltpu.VMEM((GROUP, HEAD_DIM, tq), F32),
            pltpu.VMEM((GROUP, tk, tq), F32),
            pltpu.VMEM((GROUP, tk, tq), F32),
        ],
        compiler_params=_params(("parallel", "parallel", "arbitrary")),
        name="attn",
    )(qt, k, vt)


def _gelu_tanh(y):
    return 0.5 * y * (1.0 + jnp.tanh(math.sqrt(2.0 / math.pi) * (y + 0.044715 * (y * y * y))))


def _seg_pitch(ts):
    return ts // SUBLANES + SUBLANES // 2


def _lru_scan(uc, wg_ref, bg_ref, hc_ref, a_ref, b_ref, carry_ref, *, reverse):
    ts = uc.shape[0]
    seg = ts // SUBLANES
    pitch = _seg_pitch(ts)

    @pl.when(pl.program_id(1) == 0)
    def _():
        carry_ref[...] = jnp.zeros(carry_ref.shape, F32)

    ucb = uc.astype(BF16)
    for hb in range(LRU_BLOCKS):
        cols = slice(hb * LRU_BW, (hb + 1) * LRU_BW)
        half = jnp.dot(ucb[:, cols], wg_ref[hb], preferred_element_type=F32) + bg_ref[hb]
        t_r = jnp.tanh(half[:, :LRU_BW])
        t_i = jnp.tanh(half[:, LRU_BW:])
        n = hc_ref[:, cols] * t_r + hc_ref[:, cols]
        a = jnp.exp2(n * (-LOG2_E))
        b = jnp.sqrt(jnp.tanh(n) * (1.0 + a * a)) * (0.5 * t_i + 0.5) * uc[:, cols]
        for sg in range(SUBLANES):
            a_ref[hb, pitch * sg:pitch * sg + seg, :] = a[seg * sg:seg * (sg + 1), :]
            b_ref[hb, pitch * sg:pitch * sg + seg, :] = b[seg * sg:seg * (sg + 1), :]

    unroll = min(SCAN_UNROLL, seg)

    def sweep(step, state):
        def trip(_, carried):
            base, st = carried
            for k in range(unroll):
                st = step(pl.ds(base + (unroll - 1 - k if reverse else k), SUBLANES, stride=pitch), st)
            return base + (-unroll if reverse else unroll), st
        first = jnp.int32(seg - unroll if reverse else 0)
        return lax.fori_loop(0, seg // unroll, trip, (first, state))[1]

    def local_step(rows, state):
        hs, ps = state
        new_h, new_p = [], []
        for hb in range(LRU_BLOCKS):
            a = a_ref[hb, rows, :]
            new_h.append(a * hs[hb] + b_ref[hb, rows, :])
            new_p.append(a * ps[hb])
        return tuple(new_h), tuple(new_p)

    zeros = tuple(jnp.zeros((SUBLANES, LRU_BW), F32) for _ in range(LRU_BLOCKS))
    ones = tuple(jnp.ones((SUBLANES, LRU_BW), F32) for _ in range(LRU_BLOCKS))
    h_end, p_end = sweep(local_step, (zeros, ones))

    row = lax.broadcasted_iota(jnp.int32, (SUBLANES, LRU_BW), 0)
    starts = []
    for hb in range(LRU_BLOCKS):
        cols = slice(hb * LRU_BW, (hb + 1) * LRU_BW)
        a, b = p_end[hb], h_end[hb]
        for dist in (1, 2, 4):
            if reverse:
                shift, valid = SUBLANES - dist, row < SUBLANES - dist
            else:
                shift, valid = dist, row >= dist
            a_s = jnp.where(valid, pltpu.roll(a, shift, 0), 1.0)
            b_s = jnp.where(valid, pltpu.roll(b, shift, 0), 0.0)
            b = a * b_s + b
            a = a * a_s
        carry = carry_ref[:, cols]
        ends = a * carry + b
        if reverse:
            starts.append(jnp.where(row == SUBLANES - 1, carry, pltpu.roll(ends, SUBLANES - 1, 0)))
            carry_ref[:, cols] = ends[0:1, :]
        else:
            starts.append(jnp.where(row == 0, carry, pltpu.roll(ends, 1, 0)))
            carry_ref[:, cols] = ends[SUBLANES - 1:SUBLANES, :]

    def final_step(rows, hs):
        new_h = []
        for hb in range(LRU_BLOCKS):
            h = a_ref[hb, rows, :] * hs[hb] + b_ref[hb, rows, :]
            b_ref[hb, rows, :] = h
            new_h.append(h)
        return tuple(new_h)

    sweep(final_step, tuple(starts))
    return jnp.concatenate(
        [jnp.concatenate([b_ref[hb, pitch * sg:pitch * sg + seg, :] for sg in range(SUBLANES)], axis=0)
         for hb in range(LRU_BLOCKS)], axis=1)


def _lru_fwd_body(u_ref, up_ref, un_ref, cw_ref, cb_ref, wg_ref, bg_ref, hc_ref,
                  h_ref, uc_ref, ext_ref, a_ref, b_ref, carry_ref):
    ts = u_ref.shape[1]
    i = pl.program_id(1)
    ext_ref[0:SUBLANES, :] = jnp.where(i > 0, up_ref[0], 0.0)
    ext_ref[SUBLANES:SUBLANES + ts, :] = u_ref[0]
    ext_ref[SUBLANES + ts:2 * SUBLANES + ts, :] = jnp.where(i < pl.num_programs(1) - 1, un_ref[0], 0.0)
    ext = ext_ref[...]
    uc = cb_ref[...]
    for j in range(CONV_W):
        shift = (CONV_LEFT - j) % ext.shape[0]
        tap = pltpu.roll(ext, shift, 0) if shift else ext
        uc = uc + tap[SUBLANES:SUBLANES + ts, :] * cw_ref[j:j + 1, :]
    uc_ref[0] = uc
    h_ref[0] = _lru_scan(uc, wg_ref, bg_ref, hc_ref, a_ref, b_ref, carry_ref, reverse=False)


def _lru_bwd_body(uc_ref, wg_ref, bg_ref, hc_ref, hf_ref, y_ref, o_ref, a_ref, b_ref, carry_ref):
    h_b = _lru_scan(uc_ref[0], wg_ref, bg_ref, hc_ref, a_ref, b_ref, carry_ref, reverse=True)
    o_ref[0] = ((hf_ref[0] + h_b) * _gelu_tanh(y_ref[0])).astype(o_ref.dtype)


def _lru(u, y, cw, cb, wg, bg, hc):
    b, s, w = u.shape
    ts = _tile(s, TOKEN_TILE)
    n = s // ts
    nb8 = ts // SUBLANES

    def whole(a):
        return pl.BlockSpec(a.shape, lambda bi, i: (0,) * a.ndim)

    fwd_tile = pl.BlockSpec((1, ts, w), lambda bi, i: (bi, i, 0))
    bwd_tile = pl.BlockSpec((1, ts, w), lambda bi, i: (bi, n - 1 - i, 0))
    seg_rows = SUBLANES * _seg_pitch(ts)
    scan_scratch = [pltpu.VMEM((LRU_BLOCKS, seg_rows, LRU_BW), F32), pltpu.VMEM((LRU_BLOCKS, seg_rows, LRU_BW), F32),
                    pltpu.VMEM((1, w), F32)]
    h_f, uc = pl.pallas_call(
        _lru_fwd_body,
        out_shape=(jax.ShapeDtypeStruct((b, s, w), F32), jax.ShapeDtypeStruct((b, s, w), F32)),
        grid=(b, n),
        in_specs=[
            fwd_tile,
            pl.BlockSpec((1, SUBLANES, w), lambda bi, i: (bi, jnp.maximum(i * nb8 - 1, 0), 0)),
            pl.BlockSpec((1, SUBLANES, w), lambda bi, i: (bi, jnp.minimum((i + 1) * nb8, s // SUBLANES - 1), 0)),
            whole(cw), whole(cb), whole(wg[0]), whole(bg[0]), whole(hc[0:1]),
        ],
        out_specs=(fwd_tile, fwd_tile),
        scratch_shapes=[pltpu.VMEM((ts + 2 * SUBLANES, w), F32)] + scan_scratch,
        compiler_params=_params(("parallel", "arbitrary")),
        name="lru_fwd",
    )(u, u, u, cw, cb, wg[0], bg[0], hc[0:1])
    return pl.pallas_call(
        _lru_bwd_body,
        out_shape=jax.ShapeDtypeStruct((b, s, w), BF16),
        grid=(b, n),
        in_specs=[bwd_tile, whole(wg[1]), whole(bg[1]), whole(hc[1:2]), bwd_tile, bwd_tile],
        out_specs=bwd_tile,
        scratch_shapes=scan_scratch,
        compiler_params=_params(("parallel", "arbitrary")),
        name="lru_bwd",
    )(uc, wg[1], bg[1], hc[1:2], h_f, y)


def _outproj_body(x_ref, a_ref, r_ref, wa_ref, wr_ref, o_ref):
    o_ref[...] = (x_ref[...]
                  + jnp.dot(a_ref[...], wa_ref[...], preferred_element_type=F32)
                  + jnp.dot(r_ref[...], wr_ref[...], preferred_element_type=F32))


def _outproj(x, attn, lru, wa, wr):
    t, d = x.shape
    tm = _tile(t, TOKEN_TILE)
    return pl.pallas_call(
        _outproj_body,
        out_shape=jax.ShapeDtypeStruct((t, d), F32),
        grid=(t // tm,),
        in_specs=[
            pl.BlockSpec((tm, d), lambda i: (i, 0)),
            pl.BlockSpec((tm, attn.shape[1]), lambda i: (i, 0)),
            pl.BlockSpec((tm, lru.shape[1]), lambda i: (i, 0)),
            _resident(wa.shape),
            _resident(wr.shape),
        ],
        out_specs=pl.BlockSpec((tm, d), lambda i: (i, 0)),
        compiler_params=_params(("parallel",)),
        name="outproj",
    )(x, attn, lru, wa, wr)


def _ple_body(x_ref, p_ref, g_ref, wg_ref, wp_ref, gf_ref, o_ref, *, final):
    x = x_ref[...]
    h = _rms(x, g_ref[...]).astype(BF16)
    gate = jax.nn.sigmoid(jnp.dot(h, wg_ref[...], preferred_element_type=F32))
    emb = jnp.dot(p_ref[...].astype(BF16), wp_ref[...], preferred_element_type=F32)
    x = x + gate * emb
    o_ref[...] = _rms(x, gf_ref[...]) if final else x


def _ple(x, p, g, wg, wp, gf, *, final):
    t, d = x.shape
    tm = _tile(t, TOKEN_TILE)
    return pl.pallas_call(
        functools.partial(_ple_body, final=final),
        out_shape=jax.ShapeDtypeStruct((t, d), F32),
        grid=(t // tm,),
        in_specs=[
            pl.BlockSpec((tm, d), lambda i: (i, 0)),
            pl.BlockSpec((tm, p.shape[1]), lambda i: (i, 0)),
            pl.BlockSpec((1, d), lambda i: (0, 0)),
            _resident(wg.shape),
            _resident(wp.shape),
            pl.BlockSpec((1, d), lambda i: (0, 0)),
        ],
        out_specs=pl.BlockSpec((tm, d), lambda i: (i, 0)),
        compiler_params=_params(("parallel",)),
        name="ple",
    )(x, p, g, wg, wp, gf)


def _rope_tables(seq_len):
    rows = seq_len // GRID_W
    inv = ROPE_THETA ** (-jnp.arange(0, AXIS_DIM, 2, dtype=F32) / AXIS_DIM)
    row_ang = jnp.arange(rows, dtype=F32)[:, None] * inv
    col_ang = jnp.arange(GRID_W, dtype=F32)[:, None] * inv

    def expand(fn):
        return jnp.concatenate([jnp.repeat(fn(row_ang), GRID_W, axis=0), jnp.tile(fn(col_ang), (rows, 1))], axis=-1)

    cos, sin = expand(jnp.cos), expand(jnp.sin)
    return jnp.concatenate([cos, cos], axis=-1), jnp.concatenate([-sin, sin], axis=-1)


def _deinterleave_perm():
    half = jnp.concatenate([jnp.arange(0, HEAD_DIM, 2), jnp.arange(1, HEAD_DIM, 2)])
    heads = jnp.arange(N_Q_HEADS + N_KV_HEADS)[:, None] * HEAD_DIM
    return (heads + half[None, :]).reshape(-1)


def kernel(x, p, norm_ffn1, w1_ffn1, w3_ffn1, w2_ffn1, norm_mix, w_in, q_norm, k_norm, conv_w, conv_b,
           lru_wa, lru_ba, lru_wi, lru_bi, lru_lambda, w_out, norm_ffn2, w1_ffn2, w3_ffn2, w2_ffn2,
           norm_ple, w_ple_gate, w_ple_proj, norm_final):
    b, s, d = x.shape
    depth = w_in.shape[0]
    t = b * s
    cos, sin = _rope_tables(s)
    perm = _deinterleave_perm()
    half = perm[:HEAD_DIM]
    qk_w = ATTN_W + KV_W

    x = x.reshape(t, d)
    for l in range(depth):
        row = lambda v: v.reshape(1, -1)
        x = _ffn(x, row(norm_ffn1[l]), w1_ffn1[l].astype(BF16), w3_ffn1[l].astype(BF16), w2_ffn1[l].astype(BF16))

        w_in_l = w_in[l].astype(BF16)
        w_in_l = jnp.concatenate([w_in_l[:, :qk_w][:, perm], w_in_l[:, qk_w:]], axis=1)
        qt, k, vt, u, y = _inproj(x.reshape(b, s, d), row(norm_mix[l]), w_in_l,
                                  row(q_norm[l][half]), row(k_norm[l][half]), cos, sin)
        attn = _attn(qt, k, vt)

        hc = (0.5 * LRU_C) * jax.nn.softplus(-lru_lambda[l])
        wg = (0.5 * jnp.concatenate([lru_wa[l], lru_wi[l]], axis=-1)).astype(BF16)
        bg = 0.5 * jnp.concatenate([lru_ba[l], lru_bi[l]], axis=-1)[:, :, None, :]
        lru = _lru(u, y, conv_w[l], row(conv_b[l]), wg, bg, hc)

        w_out_l = w_out[l].astype(BF16)
        x = _outproj(x, attn.reshape(t, ATTN_W), lru.reshape(t, LRU_W), w_out_l[:ATTN_W], w_out_l[ATTN_W:])

        x = _ffn(x, row(norm_ffn2[l]), w1_ffn2[l].astype(BF16), w3_ffn2[l].astype(BF16), w2_ffn2[l].astype(BF16))

        x = _ple(x, p[l].reshape(t, -1), row(norm_ple[l]), w_ple_gate[l].astype(BF16),
                 w_ple_proj[l].astype(BF16), row(norm_final), final=(l == depth - 1))
    return x.reshape(b, s, d)
```

```python
import functools
import math

import jax
import jax.numpy as jnp
from jax import lax
from jax.experimental import pallas as pl
from jax.experimental.pallas import tpu as pltpu

F32 = jnp.float32
BF16 = jnp.bfloat16

EPS = 1e-6
HEAD_DIM = 128
N_Q_HEADS = 8
N_KV_HEADS = 2
GROUP = N_Q_HEADS // N_KV_HEADS
ATTN_W = N_Q_HEADS * HEAD_DIM
KV_W = N_KV_HEADS * HEAD_DIM
LRU_BLOCKS = 8
LRU_BW = 128
LRU_W = LRU_BLOCKS * LRU_BW
LRU_C = 8.0
CONV_W = 4
CONV_LEFT = 2
GRID_W = 64
ROPE_THETA = 10000.0
AXIS_DIM = HEAD_DIM // 2

V7X_VMEM_BYTES = 64 * 1024 * 1024
VMEM_LIMIT_BYTES = V7X_VMEM_BYTES - 4 * 1024 * 1024
SUBLANES = 8
LOG2_E = math.log2(math.e)
Q_SCALE = LOG2_E / math.sqrt(HEAD_DIM)
NEG_BIG = -1e30
SAFE_SHIFT_LOG2 = 60.0
FFN_PARTS = 2
FFN_TOKEN_TILE = 512
FFN_COL_CHUNK = 1024
TOKEN_TILE = 512
ATTN_Q_TILE = 512
ATTN_KEY_BLOCK = 512
KV_UNROLL = 16
SCAN_UNROLL = 8


def _tile(n, pref):
    t = pref
    while n % t:
        t //= 2
    return t


def _params(sem):
    return pltpu.CompilerParams(dimension_semantics=sem, vmem_limit_bytes=VMEM_LIMIT_BYTES)


def _rms(x, g):
    return x * lax.rsqrt(jnp.mean(x * x, axis=-1, keepdims=True) + EPS) * g


def _resident(shape):
    nd = len(shape)
    return pl.BlockSpec(shape, lambda *_: (0,) * nd, pipeline_mode=pl.Buffered(1))


def _ffn_part_body(*refs, first, col_chunk):
    if first:
        y_ref, g_ref, w1_ref, w3_ref, w2_ref, o_ref, h_ref = refs
        h = _rms(y_ref[...], g_ref[...]).astype(BF16)
        h_ref[...] = h
    else:
        h_ref, y_ref, w1_ref, w3_ref, w2_ref, o_ref = refs
        h = h_ref[...]
    o_ref[...] = y_ref[...]
    width = w1_ref.shape[1]
    for c0 in range(0, width, col_chunk):
        cols = slice(c0, min(c0 + col_chunk, width))
        a = jnp.dot(h, w1_ref[:, cols], preferred_element_type=F32)
        b = jnp.dot(h, w3_ref[:, cols], preferred_element_type=F32)
        z = (0.5 * a * jax.nn.sigmoid(a) * b).astype(BF16)
        o_ref[...] += jnp.dot(z, w2_ref[cols, :], preferred_element_type=F32)


def _ffn(x, g, w1, w3, w2):
    t, d = x.shape
    width = w1.shape[1] // FFN_PARTS
    tm = _tile(t, FFN_TOKEN_TILE)
    tile = pl.BlockSpec((tm, d), lambda i: (i, 0))
    y, h = x, None
    for part in range(FFN_PARTS):
        first = part == 0
        weights = [
            pl.BlockSpec((d, width), lambda i, part=part: (0, part), pipeline_mode=pl.Buffered(1)),
            pl.BlockSpec((d, width), lambda i, part=part: (0, part), pipeline_mode=pl.Buffered(1)),
            pl.BlockSpec((width, d), lambda i, part=part: (part, 0), pipeline_mode=pl.Buffered(1)),
        ]
        res = jax.ShapeDtypeStruct((t, d), F32)
        out = pl.pallas_call(
            functools.partial(_ffn_part_body, first=first, col_chunk=FFN_COL_CHUNK),
            out_shape=(res, jax.ShapeDtypeStruct((t, d), BF16)) if first else res,
            grid=(t // tm,),
            in_specs=([tile, pl.BlockSpec((1, d), lambda i: (0, 0))] if first else [tile, tile]) + weights,
            out_specs=(tile, tile) if first else tile,
            compiler_params=_params(("parallel",)),
            name="ffn",
        )(*((y, g) if first else (h, y)), w1, w3, w2)
        y, h = out if first else (out, h)
    return y


def _inproj_body(x_ref, g_ref, w_ref, gq_ref, gk_ref, cos_ref, sin_ref,
                 qt_ref, k_ref, vt_ref, u_ref, y_ref):
    h = _rms(x_ref[0], g_ref[...]).astype(BF16)
    proj = jnp.dot(h, w_ref[...], preferred_element_type=F32)
    cos = cos_ref[...]
    sin = sin_ref[...]

    def norm_rope(z, g):
        z = _rms(z, g)
        return z * cos + pltpu.roll(z, HEAD_DIM // 2, 1) * sin

    for hd in range(N_Q_HEADS):
        q = norm_rope(proj[:, hd * HEAD_DIM:(hd + 1) * HEAD_DIM], gq_ref[...]) * Q_SCALE
        qt_ref[0, hd] = q.T.astype(BF16)
    for kv in range(N_KV_HEADS):
        c0 = ATTN_W + kv * HEAD_DIM
        k_ref[0, kv] = norm_rope(proj[:, c0:c0 + HEAD_DIM], gk_ref[...]).astype(BF16)
        c0 = ATTN_W + KV_W + kv * HEAD_DIM
        vt_ref[0, kv] = proj[:, c0:c0 + HEAD_DIM].T.astype(BF16)
    c0 = ATTN_W + 2 * KV_W
    u_ref[0] = proj[:, c0:c0 + LRU_W]
    y_ref[0] = proj[:, c0 + LRU_W:c0 + 2 * LRU_W]


def _inproj(x, g, w, gq, gk, cos, sin):
    b, s, d = x.shape
    tm = _tile(s, TOKEN_TILE)
    n = s // tm
    return pl.pallas_call(
        _inproj_body,
        out_shape=(
            jax.ShapeDtypeStruct((b, N_Q_HEADS, HEAD_DIM, s), BF16),
            jax.ShapeDtypeStruct((b, N_KV_HEADS, s, HEAD_DIM), BF16),
            jax.ShapeDtypeStruct((b, N_KV_HEADS, HEAD_DIM, s), BF16),
            jax.ShapeDtypeStruct((b, s, LRU_W), F32),
            jax.ShapeDtypeStruct((b, s, LRU_W), F32),
        ),
        grid=(b, n),
        in_specs=[
            pl.BlockSpec((1, tm, d), lambda bi, i: (bi, i, 0)),
            pl.BlockSpec((1, d), lambda bi, i: (0, 0)),
            _resident(w.shape),
            pl.BlockSpec((1, HEAD_DIM), lambda bi, i: (0, 0)),
            pl.BlockSpec((1, HEAD_DIM), lambda bi, i: (0, 0)),
            pl.BlockSpec((tm, HEAD_DIM), lambda bi, i: (i, 0)),
            pl.BlockSpec((tm, HEAD_DIM), lambda bi, i: (i, 0)),
        ],
        out_specs=(
            pl.BlockSpec((1, N_Q_HEADS, HEAD_DIM, tm), lambda bi, i: (bi, 0, 0, i)),
            pl.BlockSpec((1, N_KV_HEADS, tm, HEAD_DIM), lambda bi, i: (bi, 0, i, 0)),
            pl.BlockSpec((1, N_KV_HEADS, HEAD_DIM, tm), lambda bi, i: (bi, 0, 0, i)),
            pl.BlockSpec((1, tm, LRU_W), lambda bi, i: (bi, i, 0)),
            pl.BlockSpec((1, tm, LRU_W), lambda bi, i: (bi, i, 0)),
        ),
        compiler_params=_params(("parallel", "parallel")),
        name="inproj",
    )(x, g, w, gq, gk, cos, sin)


def _attn_body(qt_ref, k_ref, vt_ref, o_ref, kmax_ref, m_ref, l_ref, acc_ref, s0_ref, s1_ref, *, tk):
    s_len = k_ref.shape[2]
    n_kv = s_len // tk

    @pl.when(pl.program_id(2) == 0)
    def _():
        def key_chunk(j, best):
            kc = k_ref[0, 0, pl.ds(pl.multiple_of(j * tk, tk), tk), :].astype(F32)
            return jnp.maximum(best, jnp.sum(kc * kc, axis=1, keepdims=True))
        best = lax.fori_loop(0, n_kv, key_chunk, jnp.zeros((tk, 1), F32))
        kmax_ref[...] = jnp.max(best, axis=0, keepdims=True)

    for g in range(GROUP):
        qf = qt_ref[0, g].astype(F32)
        m_ref[g] = jnp.sqrt(jnp.sum(qf * qf, axis=0, keepdims=True) * kmax_ref[...])
    l_ref[...] = jnp.zeros(l_ref.shape, F32)
    acc_ref[...] = jnp.zeros(acc_ref.shape, F32)
    shift_is_safe = jnp.max(m_ref[...]) <= SAFE_SHIFT_LOG2

    def key_block(j):
        return k_ref[0, 0, pl.ds(pl.multiple_of(j * tk, tk), tk), :]

    def value_block(j):
        return vt_ref[0, 0, :, pl.ds(pl.multiple_of(j * tk, tk), tk)]

    def scores(j, s_ref):
        kb = key_block(j)
        for g in range(GROUP):
            s_ref[g] = jnp.dot(kb, qt_ref[0, g], preferred_element_type=F32)

    def accumulate(j, s_ref):
        vb = value_block(j)
        for g in range(GROUP):
            p = jnp.exp2(s_ref[g] - m_ref[g])
            l_ref[g] += jnp.sum(p, axis=0, keepdims=True)
            acc_ref[g] += jnp.dot(vb, p.astype(BF16), preferred_element_type=F32)

    def online_step(j, carry):
        kb, vb = key_block(j), value_block(j)
        for g in range(GROUP):
            s = jnp.dot(kb, qt_ref[0, g], preferred_element_type=F32)
            m_old = m_ref[g]
            m_new = jnp.maximum(m_old, jnp.max(s, axis=0, keepdims=True))
            alpha = jnp.exp2(m_old - m_new)
            p = jnp.exp2(s - m_new)
            l_ref[g] = alpha * l_ref[g] + jnp.sum(p, axis=0, keepdims=True)
            acc_ref[g] = alpha * acc_ref[g] + jnp.dot(vb, p.astype(BF16), preferred_element_type=F32)
            m_ref[g] = m_new
        return carry

    @pl.when(shift_is_safe)
    def _():
        bufs = (s0_ref, s1_ref)
        scores(0, s0_ref)

        def group(t, carry):
            for u in range(KV_UNROLL):
                j = KV_UNROLL * t + u
                scores(j + 1, bufs[(u + 1) % 2])
                accumulate(j, bufs[u % 2])
            return carry

        lax.fori_loop(0, n_kv // KV_UNROLL - 1, group, 0)
        for u in range(KV_UNROLL):
            j = n_kv - KV_UNROLL + u
            if u + 1 < KV_UNROLL:
                scores(j + 1, bufs[(u + 1) % 2])
            accumulate(j, bufs[u % 2])

    @pl.when(jnp.logical_not(shift_is_safe))
    def _():
        m_ref[...] = jnp.full(m_ref.shape, NEG_BIG, F32)
        lax.fori_loop(0, n_kv, online_step, 0)

    for g in range(GROUP):
        o = acc_ref[g] / l_ref[g]
        o_ref[0, :, g * HEAD_DIM:(g + 1) * HEAD_DIM] = o.T.astype(o_ref.dtype)


def _attn(qt, k, vt):
    b, _, _, s = qt.shape
    tq = _tile(s, ATTN_Q_TILE)
    tk = _tile(s // KV_UNROLL, ATTN_KEY_BLOCK)
    assert (s // tk) % KV_UNROLL == 0
    return pl.pallas_call(
        functools.partial(_attn_body, tk=tk),
        out_shape=jax.ShapeDtypeStruct((b, s, ATTN_W), BF16),
        grid=(b, N_KV_HEADS, s // tq),
        in_specs=[
            pl.BlockSpec((1, GROUP, HEAD_DIM, tq), lambda bi, kv, i: (bi, kv, 0, i)),
            pl.BlockSpec((1, 1, s, HEAD_DIM), lambda bi, kv, i: (bi, kv, 0, 0)),
            pl.BlockSpec((1, 1, HEAD_DIM, s), lambda bi, kv, i: (bi, kv, 0, 0)),
        ],
        out_specs=pl.BlockSpec((1, tq, GROUP * HEAD_DIM), lambda bi, kv, i: (bi, i, kv)),
        scratch_shapes=[
            pltpu.VMEM((1, 1), F32),
            pltpu.VMEM((GROUP, 1, tq), F32),
            pltpu.VMEM((GROUP, 1, tq), F32),
            pltpu.VMEM((GROUP, HEAD_DIM, tq), F32),
            pltpu.VMEM((GROUP, tk, tq), F32),
            pltpu.VMEM((GROUP, tk, tq), F32),
        ],
        compiler_params=_params(("parallel", "parallel", "arbitrary")),
        name="attn",
    )(qt, k, vt)


def _gelu_tanh(y):
    return 0.5 * y * (1.0 + jnp.tanh(math.sqrt(2.0 / math.pi) * (y + 0.044715 * (y * y * y))))


def _seg_pitch(ts):
    return ts // SUBLANES + SUBLANES // 2


def _lru_scan(uc, wg_ref, bg_ref, hc_ref, a_ref, b_ref, carry_ref, *, reverse):
    ts = uc.shape[0]
    seg = ts // SUBLANES
    pitch = _seg_pitch(ts)

    @pl.when(pl.program_id(1) == 0)
    def _():
        carry_ref[...] = jnp.zeros(carry_ref.shape, F32)

    ucb = uc.astype(BF16)
    for hb in range(LRU_BLOCKS):
        cols = slice(hb * LRU_BW, (hb + 1) * LRU_BW)
        half = jnp.dot(ucb[:, cols], wg_ref[hb], preferred_element_type=F32) + bg_ref[hb]
        t_r = jnp.tanh(half[:, :LRU_BW])
        t_i = jnp.tanh(half[:, LRU_BW:])
        n = hc_ref[:, cols] * t_r + hc_ref[:, cols]
        a = jnp.exp2(n * (-LOG2_E))
        b = jnp.sqrt(jnp.tanh(n) * (1.0 + a * a)) * (0.5 * t_i + 0.5) * uc[:, cols]
        for sg in range(SUBLANES):
            a_ref[hb, pitch * sg:pitch * sg + seg, :] = a[seg * sg:seg * (sg + 1), :]
            b_ref[hb, pitch * sg:pitch * sg + seg, :] = b[seg * sg:seg * (sg + 1), :]

    unroll = min(SCAN_UNROLL, seg)

    def sweep(step, state):
        def trip(_, carried):
            base, st = carried
            for k in range(unroll):
                st = step(pl.ds(base + (unroll - 1 - k if reverse else k), SUBLANES, stride=pitch), st)
            return base + (-unroll if reverse else unroll), st
        first = jnp.int32(seg - unroll if reverse else 0)
        return lax.fori_loop(0, seg // unroll, trip, (first, state))[1]

    def local_step(rows, state):
        hs, ps = state
        new_h, new_p = [], []
        for hb in range(LRU_BLOCKS):
            a = a_ref[hb, rows, :]
            new_h.append(a * hs[hb] + b_ref[hb, rows, :])
            new_p.append(a * ps[hb])
        return tuple(new_h), tuple(new_p)

    zeros = tuple(jnp.zeros((SUBLANES, LRU_BW), F32) for _ in range(LRU_BLOCKS))
    ones = tuple(jnp.ones((SUBLANES, LRU_BW), F32) for _ in range(LRU_BLOCKS))
    h_end, p_end = sweep(local_step, (zeros, ones))

    row = lax.broadcasted_iota(jnp.int32, (SUBLANES, LRU_BW), 0)
    starts = []
    for hb in range(LRU_BLOCKS):
        cols = slice(hb * LRU_BW, (hb + 1) * LRU_BW)
        a, b = p_end[hb], h_end[hb]
        for dist in (1, 2, 4):
            if reverse:
                shift, valid = SUBLANES - dist, row < SUBLANES - dist
            else:
                shift, valid = dist, row >= dist
            a_s = jnp.where(valid, pltpu.roll(a, shift, 0), 1.0)
            b_s = jnp.where(valid, pltpu.roll(b, shift, 0), 0.0)
            b = a * b_s + b
            a = a * a_s
        carry = carry_ref[:, cols]
        ends = a * carry + b
        if reverse:
            starts.append(jnp.where(row == SUBLANES - 1, carry, pltpu.roll(ends, SUBLANES - 1, 0)))
            carry_ref[:, cols] = ends[0:1, :]
        else:
            starts.append(jnp.where(row == 0, carry, pltpu.roll(ends, 1, 0)))
            carry_ref[:, cols] = ends[SUBLANES - 1:SUBLANES, :]

    def final_step(rows, hs):
        new_h = []
        for hb in range(LRU_BLOCKS):
            h = a_ref[hb, rows, :] * hs[hb] + b_ref[hb, rows, :]
            b_ref[hb, rows, :] = h
            new_h.append(h)
        return tuple(new_h)

    sweep(final_step, tuple(starts))
    return jnp.concatenate(
        [jnp.concatenate([b_ref[hb, pitch * sg:pitch * sg + seg, :] for sg in range(SUBLANES)], axis=0)
         for hb in range(LRU_BLOCKS)], axis=1)


def _lru_fwd_body(u_ref, up_ref, un_ref, cw_ref, cb_ref, wg_ref, bg_ref, hc_ref,
                  h_ref, uc_ref, ext_ref, a_ref, b_ref, carry_ref):
    ts = u_ref.shape[1]
    i = pl.program_id(1)
    ext_ref[0:SUBLANES, :] = jnp.where(i > 0, up_ref[0], 0.0)
    ext_ref[SUBLANES:SUBLANES + ts, :] = u_ref[0]
    ext_ref[SUBLANES + ts:2 * SUBLANES + ts, :] = jnp.where(i < pl.num_programs(1) - 1, un_ref[0], 0.0)
    ext = ext_ref[...]
    uc = cb_ref[...]
    for j in range(CONV_W):
        shift = (CONV_LEFT - j) % ext.shape[0]
        tap = pltpu.roll(ext, shift, 0) if shift else ext
        uc = uc + tap[SUBLANES:SUBLANES + ts, :] * cw_ref[j:j + 1, :]
    uc_ref[0] = uc
    h_ref[0] = _lru_scan(uc, wg_ref, bg_ref, hc_ref, a_ref, b_ref, carry_ref, reverse=False)


def _lru_bwd_body(uc_ref, wg_ref, bg_ref, hc_ref, hf_ref, y_ref, o_ref, a_ref, b_ref, carry_ref):
    h_b = _lru_scan(uc_ref[0], wg_ref, bg_ref, hc_ref, a_ref, b_ref, carry_ref, reverse=True)
    o_ref[0] = ((hf_ref[0] + h_b) * _gelu_tanh(y_ref[0])).astype(o_ref.dtype)


def _lru(u, y, cw, cb, wg, bg, hc):
    b, s, w = u.shape
    ts = _tile(s, TOKEN_TILE)
    n = s // ts
    nb8 = ts // SUBLANES

    def whole(a):
        return pl.BlockSpec(a.shape, lambda bi, i: (0,) * a.ndim)

    fwd_tile = pl.BlockSpec((1, ts, w), lambda bi, i: (bi, i, 0))
    bwd_tile = pl.BlockSpec((1, ts, w), lambda bi, i: (bi, n - 1 - i, 0))
    seg_rows = SUBLANES * _seg_pitch(ts)
    scan_scratch = [pltpu.VMEM((LRU_BLOCKS, seg_rows, LRU_BW), F32), pltpu.VMEM((LRU_BLOCKS, seg_rows, LRU_BW), F32),
                    pltpu.VMEM((1, w), F32)]
    h_f, uc = pl.pallas_call(
        _lru_fwd_body,
        out_shape=(jax.ShapeDtypeStruct((b, s, w), F32), jax.ShapeDtypeStruct((b, s, w), F32)),
        grid=(b, n),
        in_specs=[
            fwd_tile,
            pl.BlockSpec((1, SUBLANES, w), lambda bi, i: (bi, jnp.maximum(i * nb8 - 1, 0), 0)),
            pl.BlockSpec((1, SUBLANES, w), lambda bi, i: (bi, jnp.minimum((i + 1) * nb8, s // SUBLANES - 1), 0)),
            whole(cw), whole(cb), whole(wg[0]), whole(bg[0]), whole(hc[0:1]),
        ],
        out_specs=(fwd_tile, fwd_tile),
        scratch_shapes=[pltpu.VMEM((ts + 2 * SUBLANES, w), F32)] + scan_scratch,
        compiler_params=_params(("parallel", "arbitrary")),
        name="lru_fwd",
    )(u, u, u, cw, cb, wg[0], bg[0], hc[0:1])
    return pl.pallas_call(
        _lru_bwd_body,
        out_shape=jax.ShapeDtypeStruct((b, s, w), BF16),
        grid=(b, n),
        in_specs=[bwd_tile, whole(wg[1]), whole(bg[1]), whole(hc[1:2]), bwd_tile, bwd_tile],
        out_specs=bwd_tile,
        scratch_shapes=scan_scratch,
        compiler_params=_params(("parallel", "arbitrary")),
        name="lru_bwd",
    )(uc, wg[1], bg[1], hc[1:2], h_f, y)


def _outproj_body(x_ref, a_ref, r_ref, wa_ref, wr_ref, o_ref):
    o_ref[...] = (x_ref[...]
                  + jnp.dot(a_ref[...], wa_ref[...], preferred_element_type=F32)
                  + jnp.dot(r_ref[...], wr_ref[...], preferred_element_type=F32))


def _outproj(x, attn, lru, wa, wr):
    t, d = x.shape
    tm = _tile(t, TOKEN_TILE)
    return pl.pallas_call(
        _outproj_body,
        out_shape=jax.ShapeDtypeStruct((t, d), F32),
        grid=(t // tm,),
        in_specs=[
            pl.BlockSpec((tm, d), lambda i: (i, 0)),
            pl.BlockSpec((tm, attn.shape[1]), lambda i: (i, 0)),
            pl.BlockSpec((tm, lru.shape[1]), lambda i: (i, 0)),
            _resident(wa.shape),
            _resident(wr.shape),
        ],
        out_specs=pl.BlockSpec((tm, d), lambda i: (i, 0)),
        compiler_params=_params(("parallel",)),
        name="outproj",
    )(x, attn, lru, wa, wr)


def _ple_body(x_ref, p_ref, g_ref, wg_ref, wp_ref, gf_ref, o_ref, *, final):
    x = x_ref[...]
    h = _rms(x, g_ref[...]).astype(BF16)
    gate = jax.nn.sigmoid(jnp.dot(h, wg_ref[...], preferred_element_type=F32))
    emb = jnp.dot(p_ref[...].astype(BF16), wp_ref[...], preferred_element_type=F32)
    x = x + gate * emb
    o_ref[...] = _rms(x, gf_ref[...]) if final else x


def _ple(x, p, g, wg, wp, gf, *, final):
    t, d = x.shape
    tm = _tile(t, TOKEN_TILE)
    return pl.pallas_call(
        functools.partial(_ple_body, final=final),
        out_shape=jax.ShapeDtypeStruct((t, d), F32),
        grid=(t // tm,),
        in_specs=[
            pl.BlockSpec((tm, d), lambda i: (i, 0)),
            pl.BlockSpec((tm, p.shape[1]), lambda i: (i, 0)),
            pl.BlockSpec((1, d), lambda i: (0, 0)),
            _resident(wg.shape),
            _resident(wp.shape),
            pl.BlockSpec((1, d), lambda i: (0, 0)),
        ],
        out_specs=pl.BlockSpec((tm, d), lambda i: (i, 0)),
        compiler_params=_params(("parallel",)),
        name="ple",
    )(x, p, g, wg, wp, gf)


def _rope_tables(seq_len):
    rows = seq_len // GRID_W
    inv = ROPE_THETA ** (-jnp.arange(0, AXIS_DIM, 2, dtype=F32) / AXIS_DIM)
    row_ang = jnp.arange(rows, dtype=F32)[:, None] * inv
    col_ang = jnp.arange(GRID_W, dtype=F32)[:, None] * inv

    def expand(fn):
        return jnp.concatenate([jnp.repeat(fn(row_ang), GRID_W, axis=0), jnp.tile(fn(col_ang), (rows, 1))], axis=-1)

    cos, sin = expand(jnp.cos), expand(jnp.sin)
    return jnp.concatenate([cos, cos], axis=-1), jnp.concatenate([-sin, sin], axis=-1)


def _deinterleave_perm():
    half = jnp.concatenate([jnp.arange(0, HEAD_DIM, 2), jnp.arange(1, HEAD_DIM, 2)])
    heads = jnp.arange(N_Q_HEADS + N_KV_HEADS)[:, None] * HEAD_DIM
    return (heads + half[None, :]).reshape(-1)


def kernel(x, p, norm_ffn1, w1_ffn1, w3_ffn1, w2_ffn1, norm_mix, w_in, q_norm, k_norm, conv_w, conv_b,
           lru_wa, lru_ba, lru_wi, lru_bi, lru_lambda, w_out, norm_ffn2, w1_ffn2, w3_ffn2, w2_ffn2,
           norm_ple, w_ple_gate, w_ple_proj, norm_final):
    b, s, d = x.shape
    depth = w_in.shape[0]
    t = b * s
    cos, sin = _rope_tables(s)
    perm = _deinterleave_perm()
    half = perm[:HEAD_DIM]
    qk_w = ATTN_W + KV_W

    x = x.reshape(t, d)
    for l in range(depth):
        row = lambda v: v.reshape(1, -1)
        x = _ffn(x, row(norm_ffn1[l]), w1_ffn1[l].astype(BF16), w3_ffn1[l].astype(BF16), w2_ffn1[l].astype(BF16))

        w_in_l = w_in[l].astype(BF16)
        w_in_l = jnp.concatenate([w_in_l[:, :qk_w][:, perm], w_in_l[:, qk_w:]], axis=1)
        qt, k, vt, u, y = _inproj(x.reshape(b, s, d), row(norm_mix[l]), w_in_l,
                                  row(q_norm[l][half]), row(k_norm[l][half]), cos, sin)
        attn = _attn(qt, k, vt)

        hc = (0.5 * LRU_C) * jax.nn.softplus(-lru_lambda[l])
        wg = (0.5 * jnp.concatenate([lru_wa[l], lru_wi[l]], axis=-1)).astype(BF16)
        bg = 0.5 * jnp.concatenate([lru_ba[l], lru_bi[l]], axis=-1)[:, :, None, :]
        lru = _lru(u, y, conv_w[l], row(conv_b[l]), wg, bg, hc)

        w_out_l = w_out[l].astype(BF16)
        x = _outproj(x, attn.reshape(t, ATTN_W), lru.reshape(t, LRU_W), w_out_l[:ATTN_W], w_out_l[ATTN_W:])

        x = _ffn(x, row(norm_ffn2[l]), w1_ffn2[l].astype(BF16), w3_ffn2[l].astype(BF16), w2_ffn2[l].astype(BF16))

        x = _ple(x, p[l].reshape(t, -1), row(norm_ple[l]), w_ple_gate[l].astype(BF16),
                 w_ple_proj[l].astype(BF16), row(norm_final), final=(l == depth - 1))
    return x.reshape(b, s, d)
```

```python
import functools
import math

import jax
import jax.numpy as jnp
from jax import lax
from jax.experimental import pallas as pl
from jax.experimental.pallas import tpu as pltpu

F32 = jnp.float32
BF16 = jnp.bfloat16

EPS = 1e-6
HEAD_DIM = 128
N_Q_HEADS = 8
N_KV_HEADS = 2
GROUP = N_Q_HEADS // N_KV_HEADS
ATTN_W = N_Q_HEADS * HEAD_DIM
KV_W = N_KV_HEADS * HEAD_DIM
LRU_BLOCKS = 8
LRU_BW = 128
LRU_W = LRU_BLOCKS * LRU_BW
LRU_C = 8.0
CONV_W = 4
CONV_LEFT = 2
GRID_W = 64
ROPE_THETA = 10000.0
AXIS_DIM = HEAD_DIM // 2

V7X_VMEM_BYTES = 64 * 1024 * 1024
VMEM_LIMIT_BYTES = V7X_VMEM_BYTES - 4 * 1024 * 1024
SUBLANES = 8
LOG2_E = math.log2(math.e)
Q_SCALE = LOG2_E / math.sqrt(HEAD_DIM)
NEG_BIG = -1e30
SAFE_SHIFT_LOG2 = 60.0
FFN_PARTS = 2
FFN_TOKEN_TILE = 512
FFN_COL_CHUNK = 1024
TOKEN_TILE = 512
ATTN_Q_TILE = 512
ATTN_KEY_BLOCK = 1024
KV_UNROLL = 16
SCAN_UNROLL = 8


def _tile(n, pref):
    t = pref
    while n % t:
        t //= 2
    return t


def _params(sem):
    return pltpu.CompilerParams(dimension_semantics=sem, vmem_limit_bytes=VMEM_LIMIT_BYTES)


def _rms(x, g):
    return x * lax.rsqrt(jnp.mean(x * x, axis=-1, keepdims=True) + EPS) * g


def _resident(shape):
    nd = len(shape)
    return pl.BlockSpec(shape, lambda *_: (0,) * nd, pipeline_mode=pl.Buffered(1))


def _ffn_part_body(*refs, first, col_chunk):
    if first:
        y_ref, g_ref, w1_ref, w3_ref, w2_ref, o_ref, h_ref = refs
        h = _rms(y_ref[...], g_ref[...]).astype(BF16)
        h_ref[...] = h
    else:
        h_ref, y_ref, w1_ref, w3_ref, w2_ref, o_ref = refs
        h = h_ref[...]
    o_ref[...] = y_ref[...]
    width = w1_ref.shape[1]
    for c0 in range(0, width, col_chunk):
        cols = slice(c0, min(c0 + col_chunk, width))
        a = jnp.dot(h, w1_ref[:, cols], preferred_element_type=F32)
        b = jnp.dot(h, w3_ref[:, cols], preferred_element_type=F32)
        z = (0.5 * a * jax.nn.sigmoid(a) * b).astype(BF16)
        o_ref[...] += jnp.dot(z, w2_ref[cols, :], preferred_element_type=F32)


def _ffn(x, g, w1, w3, w2):
    t, d = x.shape
    width = w1.shape[1] // FFN_PARTS
    tm = _tile(t, FFN_TOKEN_TILE)
    tile = pl.BlockSpec((tm, d), lambda i: (i, 0))
    y, h = x, None
    for part in range(FFN_PARTS):
        first = part == 0
        weights = [
            pl.BlockSpec((d, width), lambda i, part=part: (0, part), pipeline_mode=pl.Buffered(1)),
            pl.BlockSpec((d, width), lambda i, part=part: (0, part), pipeline_mode=pl.Buffered(1)),
            pl.BlockSpec((width, d), lambda i, part=part: (part, 0), pipeline_mode=pl.Buffered(1)),
        ]
        res = jax.ShapeDtypeStruct((t, d), F32)
        out = pl.pallas_call(
            functools.partial(_ffn_part_body, first=first, col_chunk=FFN_COL_CHUNK),
            out_shape=(res, jax.ShapeDtypeStruct((t, d), BF16)) if first else res,
            grid=(t // tm,),
            in_specs=([tile, pl.BlockSpec((1, d), lambda i: (0, 0))] if first else [tile, tile]) + weights,
            out_specs=(tile, tile) if first else tile,
            compiler_params=_params(("parallel",)),
            name="ffn",
        )(*((y, g) if first else (h, y)), w1, w3, w2)
        y, h = out if first else (out, h)
    return y


def _inproj_body(x_ref, g_ref, w_ref, gq_ref, gk_ref, cos_ref, sin_ref,
                 qt_ref, k_ref, vt_ref, u_ref, y_ref):
    h = _rms(x_ref[0], g_ref[...]).astype(BF16)
    proj = jnp.dot(h, w_ref[...], preferred_element_type=F32)
    cos = cos_ref[...]
    sin = sin_ref[...]

    def norm_rope(z, g):
        z = _rms(z, g)
        return z * cos + pltpu.roll(z, HEAD_DIM // 2, 1) * sin

    for hd in range(N_Q_HEADS):
        q = norm_rope(proj[:, hd * HEAD_DIM:(hd + 1) * HEAD_DIM], gq_ref[...]) * Q_SCALE
        qt_ref[0, hd] = q.T.astype(BF16)
    for kv in range(N_KV_HEADS):
        c0 = ATTN_W + kv * HEAD_DIM
        k_ref[0, kv] = norm_rope(proj[:, c0:c0 + HEAD_DIM], gk_ref[...]).astype(BF16)
        c0 = ATTN_W + KV_W + kv * HEAD_DIM
        vt_ref[0, kv] = proj[:, c0:c0 + HEAD_DIM].T.astype(BF16)
    c0 = ATTN_W + 2 * KV_W
    u_ref[0] = proj[:, c0:c0 + LRU_W]
    y_ref[0] = proj[:, c0 + LRU_W:c0 + 2 * LRU_W]


def _inproj(x, g, w, gq, gk, cos, sin):
    b, s, d = x.shape
    tm = _tile(s, TOKEN_TILE)
    n = s // tm
    return pl.pallas_call(
        _inproj_body,
        out_shape=(
            jax.ShapeDtypeStruct((b, N_Q_HEADS, HEAD_DIM, s), BF16),
            jax.ShapeDtypeStruct((b, N_KV_HEADS, s, HEAD_DIM), BF16),
            jax.ShapeDtypeStruct((b, N_KV_HEADS, HEAD_DIM, s), BF16),
            jax.ShapeDtypeStruct((b, s, LRU_W), F32),
            jax.ShapeDtypeStruct((b, s, LRU_W), F32),
        ),
        grid=(b, n),
        in_specs=[
            pl.BlockSpec((1, tm, d), lambda bi, i: (bi, i, 0)),
            pl.BlockSpec((1, d), lambda bi, i: (0, 0)),
            _resident(w.shape),
            pl.BlockSpec((1, HEAD_DIM), lambda bi, i: (0, 0)),
            pl.BlockSpec((1, HEAD_DIM), lambda bi, i: (0, 0)),
            pl.BlockSpec((tm, HEAD_DIM), lambda bi, i: (i, 0)),
            pl.BlockSpec((tm, HEAD_DIM), lambda bi, i: (i, 0)),
        ],
        out_specs=(
            pl.BlockSpec((1, N_Q_HEADS, HEAD_DIM, tm), lambda bi, i: (bi, 0, 0, i)),
            pl.BlockSpec((1, N_KV_HEADS, tm, HEAD_DIM), lambda bi, i: (bi, 0, i, 0)),
            pl.BlockSpec((1, N_KV_HEADS, HEAD_DIM, tm), lambda bi, i: (bi, 0, 0, i)),
            pl.BlockSpec((1, tm, LRU_W), lambda bi, i: (bi, i, 0)),
            pl.BlockSpec((1, tm, LRU_W), lambda bi, i: (bi, i, 0)),
        ),
        compiler_params=_params(("parallel", "parallel")),
        name="inproj",
    )(x, g, w, gq, gk, cos, sin)


def _attn_body(qt_ref, k_ref, vt_ref, o_ref, kmax_ref, m_ref, l_ref, acc_ref, s0_ref, s1_ref, *, tk):
    s_len = k_ref.shape[2]
    n_kv = s_len // tk

    @pl.when(pl.program_id(2) == 0)
    def _():
        def key_chunk(j, best):
            kc = k_ref[0, 0, pl.ds(pl.multiple_of(j * tk, tk), tk), :].astype(F32)
            return jnp.maximum(best, jnp.sum(kc * kc, axis=1, keepdims=True))
        best = lax.fori_loop(0, n_kv, key_chunk, jnp.zeros((tk, 1), F32))
        kmax_ref[...] = jnp.max(best, axis=0, keepdims=True)

    for g in range(GROUP):
        qf = qt_ref[0, g].astype(F32)
        m_ref[g] = jnp.sqrt(jnp.sum(qf * qf, axis=0, keepdims=True) * kmax_ref[...])
    l_ref[...] = jnp.zeros(l_ref.shape, F32)
    acc_ref[...] = jnp.zeros(acc_ref.shape, F32)
    shift_is_safe = jnp.max(m_ref[...]) <= SAFE_SHIFT_LOG2

    def key_block(j):
        return k_ref[0, 0, pl.ds(pl.multiple_of(j * tk, tk), tk), :]

    def value_block(j):
        return vt_ref[0, 0, :, pl.ds(pl.multiple_of(j * tk, tk), tk)]

    def scores(j, s_ref):
        kb = key_block(j)
        for g in range(GROUP):
            s_ref[g] = jnp.dot(kb, qt_ref[0, g], preferred_element_type=F32)

    def accumulate(j, s_ref):
        vb = value_block(j)
        for g in range(GROUP):
            p = jnp.exp2(s_ref[g] - m_ref[g])
            l_ref[g] += jnp.sum(p, axis=0, keepdims=True)
            acc_ref[g] += jnp.dot(vb, p.astype(BF16), preferred_element_type=F32)

    def online_step(j, carry):
        kb, vb = key_block(j), value_block(j)
        for g in range(GROUP):
            s = jnp.dot(kb, qt_ref[0, g], preferred_element_type=F32)
            m_old = m_ref[g]
            m_new = jnp.maximum(m_old, jnp.max(s, axis=0, keepdims=True))
            alpha = jnp.exp2(m_old - m_new)
            p = jnp.exp2(s - m_new)
            l_ref[g] = alpha * l_ref[g] + jnp.sum(p, axis=0, keepdims=True)
            acc_ref[g] = alpha * acc_ref[g] + jnp.dot(vb, p.astype(BF16), preferred_element_type=F32)
            m_ref[g] = m_new
        return carry

    @pl.when(shift_is_safe)
    def _():
        bufs = (s0_ref, s1_ref)
        scores(0, s0_ref)

        def group(t, carry):
            for u in range(KV_UNROLL):
                j = KV_UNROLL * t + u
                scores(j + 1, bufs[(u + 1) % 2])
                accumulate(j, bufs[u % 2])
            return carry

        lax.fori_loop(0, n_kv // KV_UNROLL - 1, group, 0)
        for u in range(KV_UNROLL):
            j = n_kv - KV_UNROLL + u
            if u + 1 < KV_UNROLL:
                scores(j + 1, bufs[(u + 1) % 2])
            accumulate(j, bufs[u % 2])

    @pl.when(jnp.logical_not(shift_is_safe))
    def _():
        m_ref[...] = jnp.full(m_ref.shape, NEG_BIG, F32)
        lax.fori_loop(0, n_kv, online_step, 0)

    for g in range(GROUP):
        o = acc_ref[g] / l_ref[g]
        o_ref[0, :, g * HEAD_DIM:(g + 1) * HEAD_DIM] = o.T.astype(o_ref.dtype)


def _attn(qt, k, vt):
    b, _, _, s = qt.shape
    tq = _tile(s, ATTN_Q_TILE)
    tk = _tile(s // KV_UNROLL, ATTN_KEY_BLOCK)
    assert (s // tk) % KV_UNROLL == 0
    return pl.pallas_call(
        functools.partial(_attn_body, tk=tk),
        out_shape=jax.ShapeDtypeStruct((b, s, ATTN_W), BF16),
        grid=(b, N_KV_HEADS, s // tq),
        in_specs=[
            pl.BlockSpec((1, GROUP, HEAD_DIM, tq), lambda bi, kv, i: (bi, kv, 0, i)),
            pl.BlockSpec((1, 1, s, HEAD_DIM), lambda bi, kv, i: (bi, kv, 0, 0)),
            pl.BlockSpec((1, 1, HEAD_DIM, s), lambda bi, kv, i: (bi, kv, 0, 0)),
        ],
        out_specs=pl.BlockSpec((1, tq, GROUP * HEAD_DIM), lambda bi, kv, i: (bi, i, kv)),
        scratch_shapes=[
            pltpu.VMEM((1, 1), F32),
            pltpu.VMEM((GROUP, 1, tq), F32),
            pltpu.VMEM((GROUP, 1, tq), F32),
            pltpu.VMEM((GROUP, HEAD_DIM, tq), F32),
            pltpu.VMEM((GROUP, tk, tq), F32),
            pltpu.VMEM((GROUP, tk, tq), F32),
        ],
        compiler_params=_params(("parallel", "parallel", "arbitrary")),
        name="attn",
    )(qt, k, vt)


def _gelu_tanh(y):
    return 0.5 * y * (1.0 + jnp.tanh(math.sqrt(2.0 / math.pi) * (y + 0.044715 * (y * y * y))))


def _seg_pitch(ts):
    return ts // SUBLANES + SUBLANES // 2


def _lru_scan(uc, wg_ref, bg_ref, hc_ref, a_ref, b_ref, carry_ref, *, reverse):
    ts = uc.shape[0]
    seg = ts // SUBLANES
    pitch = _seg_pitch(ts)

    @pl.when(pl.program_id(1) == 0)
    def _():
        carry_ref[...] = jnp.zeros(carry_ref.shape, F32)

    ucb = uc.astype(BF16)
    for hb in range(LRU_BLOCKS):
        cols = slice(hb * LRU_BW, (hb + 1) * LRU_BW)
        half = jnp.dot(ucb[:, cols], wg_ref[hb], preferred_element_type=F32) + bg_ref[hb]
        t_r = jnp.tanh(half[:, :LRU_BW])
        t_i = jnp.tanh(half[:, LRU_BW:])
        n = hc_ref[:, cols] * t_r + hc_ref[:, cols]
        a = jnp.exp2(n * (-LOG2_E))
        b = jnp.sqrt(jnp.tanh(n) * (1.0 + a * a)) * (0.5 * t_i + 0.5) * uc[:, cols]
        for sg in range(SUBLANES):
            a_ref[hb, pitch * sg:pitch * sg + seg, :] = a[seg * sg:seg * (sg + 1), :]
            b_ref[hb, pitch * sg:pitch * sg + seg, :] = b[seg * sg:seg * (sg + 1), :]

    unroll = min(SCAN_UNROLL, seg)

    def sweep(step, state):
        def trip(_, carried):
            base, st = carried
            for k in range(unroll):
                st = step(pl.ds(base + (unroll - 1 - k if reverse else k), SUBLANES, stride=pitch), st)
            return base + (-unroll if reverse else unroll), st
        first = jnp.int32(seg - unroll if reverse else 0)
        return lax.fori_loop(0, seg // unroll, trip, (first, state))[1]

    def local_step(rows, state):
        hs, ps = state
        new_h, new_p = [], []
        for hb in range(LRU_BLOCKS):
            a = a_ref[hb, rows, :]
            new_h.append(a * hs[hb] + b_ref[hb, rows, :])
            new_p.append(a * ps[hb])
        return tuple(new_h), tuple(new_p)

    zeros = tuple(jnp.zeros((SUBLANES, LRU_BW), F32) for _ in range(LRU_BLOCKS))
    ones = tuple(jnp.ones((SUBLANES, LRU_BW), F32) for _ in range(LRU_BLOCKS))
    h_end, p_end = sweep(local_step, (zeros, ones))

    row = lax.broadcasted_iota(jnp.int32, (SUBLANES, LRU_BW), 0)
    starts = []
    for hb in range(LRU_BLOCKS):
        cols = slice(hb * LRU_BW, (hb + 1) * LRU_BW)
        a, b = p_end[hb], h_end[hb]
        for dist in (1, 2, 4):
            if reverse:
                shift, valid = SUBLANES - dist, row < SUBLANES - dist
            else:
                shift, valid = dist, row >= dist
            a_s = jnp.where(valid, pltpu.roll(a, shift, 0), 1.0)
            b_s = jnp.where(valid, pltpu.roll(b, shift, 0), 0.0)
            b = a * b_s + b
            a = a * a_s
        carry = carry_ref[:, cols]
        ends = a * carry + b
        if reverse:
            starts.append(jnp.where(row == SUBLANES - 1, carry, pltpu.roll(ends, SUBLANES - 1, 0)))
            carry_ref[:, cols] = ends[0:1, :]
        else:
            starts.append(jnp.where(row == 0, carry, pltpu.roll(ends, 1, 0)))
            carry_ref[:, cols] = ends[SUBLANES - 1:SUBLANES, :]

    def final_step(rows, hs):
        new_h = []
        for hb in range(LRU_BLOCKS):
            h = a_ref[hb, rows, :] * hs[hb] + b_ref[hb, rows, :]
            b_ref[hb, rows, :] = h
            new_h.append(h)
        return tuple(new_h)

    sweep(final_step, tuple(starts))
    return jnp.concatenate(
        [jnp.concatenate([b_ref[hb, pitch * sg:pitch * sg + seg, :] for sg in range(SUBLANES)], axis=0)
         for hb in range(LRU_BLOCKS)], axis=1)


def _lru_fwd_body(u_ref, up_ref, un_ref, cw_ref, cb_ref, wg_ref, bg_ref, hc_ref,
                  h_ref, uc_ref, ext_ref, a_ref, b_ref, carry_ref):
    ts = u_ref.shape[1]
    i = pl.program_id(1)
    ext_ref[0:SUBLANES, :] = jnp.where(i > 0, up_ref[0], 0.0)
    ext_ref[SUBLANES:SUBLANES + ts, :] = u_ref[0]
    ext_ref[SUBLANES + ts:2 * SUBLANES + ts, :] = jnp.where(i < pl.num_programs(1) - 1, un_ref[0], 0.0)
    ext = ext_ref[...]
    uc = cb_ref[...]
    for j in range(CONV_W):
        shift = (CONV_LEFT - j) % ext.shape[0]
        tap = pltpu.roll(ext, shift, 0) if shift else ext
        uc = uc + tap[SUBLANES:SUBLANES + ts, :] * cw_ref[j:j + 1, :]
    uc_ref[0] = uc
    h_ref[0] = _lru_scan(uc, wg_ref, bg_ref, hc_ref, a_ref, b_ref, carry_ref, reverse=False)


def _lru_bwd_body(uc_ref, wg_ref, bg_ref, hc_ref, hf_ref, y_ref, o_ref, a_ref, b_ref, carry_ref):
    h_b = _lru_scan(uc_ref[0], wg_ref, bg_ref, hc_ref, a_ref, b_ref, carry_ref, reverse=True)
    o_ref[0] = ((hf_ref[0] + h_b) * _gelu_tanh(y_ref[0])).astype(o_ref.dtype)


def _lru(u, y, cw, cb, wg, bg, hc):
    b, s, w = u.shape
    ts = _tile(s, TOKEN_TILE)
    n = s // ts
    nb8 = ts // SUBLANES

    def whole(a):
        return pl.BlockSpec(a.shape, lambda bi, i: (0,) * a.ndim)

    fwd_tile = pl.BlockSpec((1, ts, w), lambda bi, i: (bi, i, 0))
    bwd_tile = pl.BlockSpec((1, ts, w), lambda bi, i: (bi, n - 1 - i, 0))
    seg_rows = SUBLANES * _seg_pitch(ts)
    scan_scratch = [pltpu.VMEM((LRU_BLOCKS, seg_rows, LRU_BW), F32), pltpu.VMEM((LRU_BLOCKS, seg_rows, LRU_BW), F32),
                    pltpu.VMEM((1, w), F32)]
    h_f, uc = pl.pallas_call(
        _lru_fwd_body,
        out_shape=(jax.ShapeDtypeStruct((b, s, w), F32), jax.ShapeDtypeStruct((b, s, w), F32)),
        grid=(b, n),
        in_specs=[
            fwd_tile,
            pl.BlockSpec((1, SUBLANES, w), lambda bi, i: (bi, jnp.maximum(i * nb8 - 1, 0), 0)),
            pl.BlockSpec((1, SUBLANES, w), lambda bi, i: (bi, jnp.minimum((i + 1) * nb8, s // SUBLANES - 1), 0)),
            whole(cw), whole(cb), whole(wg[0]), whole(bg[0]), whole(hc[0:1]),
        ],
        out_specs=(fwd_tile, fwd_tile),
        scratch_shapes=[pltpu.VMEM((ts + 2 * SUBLANES, w), F32)] + scan_scratch,
        compiler_params=_params(("parallel", "arbitrary")),
        name="lru_fwd",
    )(u, u, u, cw, cb, wg[0], bg[0], hc[0:1])
    return pl.pallas_call(
        _lru_bwd_body,
        out_shape=jax.ShapeDtypeStruct((b, s, w), BF16),
        grid=(b, n),
        in_specs=[bwd_tile, whole(wg[1]), whole(bg[1]), whole(hc[1:2]), bwd_tile, bwd_tile],
        out_specs=bwd_tile,
        scratch_shapes=scan_scratch,
        compiler_params=_params(("parallel", "arbitrary")),
        name="lru_bwd",
    )(uc, wg[1], bg[1], hc[1:2], h_f, y)


def _outproj_body(x_ref, a_ref, r_ref, wa_ref, wr_ref, o_ref):
    o_ref[...] = (x_ref[...]
                  + jnp.dot(a_ref[...], wa_ref[...], preferred_element_type=F32)
                  + jnp.dot(r_ref[...], wr_ref[...], preferred_element_type=F32))


def _outproj(x, attn, lru, wa, wr):
    t, d = x.shape
    tm = _tile(t, TOKEN_TILE)
    return pl.pallas_call(
        _outproj_body,
        out_shape=jax.ShapeDtypeStruct((t, d), F32),
        grid=(t // tm,),
        in_specs=[
            pl.BlockSpec((tm, d), lambda i: (i, 0)),
            pl.BlockSpec((tm, attn.shape[1]), lambda i: (i, 0)),
            pl.BlockSpec((tm, lru.shape[1]), lambda i: (i, 0)),
            _resident(wa.shape),
            _resident(wr.shape),
        ],
        out_specs=pl.BlockSpec((tm, d), lambda i: (i, 0)),
        compiler_params=_params(("parallel",)),
        name="outproj",
    )(x, attn, lru, wa, wr)


def _ple_body(x_ref, p_ref, g_ref, wg_ref, wp_ref, gf_ref, o_ref, *, final):
    x = x_ref[...]
    h = _rms(x, g_ref[...]).astype(BF16)
    gate = jax.nn.sigmoid(jnp.dot(h, wg_ref[...], preferred_element_type=F32))
    emb = jnp.dot(p_ref[...].astype(BF16), wp_ref[...], preferred_element_type=F32)
    x = x + gate * emb
    o_ref[...] = _rms(x, gf_ref[...]) if final else x


def _ple(x, p, g, wg, wp, gf, *, final):
    t, d = x.shape
    tm = _tile(t, TOKEN_TILE)
    return pl.pallas_call(
        functools.partial(_ple_body, final=final),
        out_shape=jax.ShapeDtypeStruct((t, d), F32),
        grid=(t // tm,),
        in_specs=[
            pl.BlockSpec((tm, d), lambda i: (i, 0)),
            pl.BlockSpec((tm, p.shape[1]), lambda i: (i, 0)),
            pl.BlockSpec((1, d), lambda i: (0, 0)),
            _resident(wg.shape),
            _resident(wp.shape),
            pl.BlockSpec((1, d), lambda i: (0, 0)),
        ],
        out_specs=pl.BlockSpec((tm, d), lambda i: (i, 0)),
        compiler_params=_params(("parallel",)),
        name="ple",
    )(x, p, g, wg, wp, gf)


def _rope_tables(seq_len):
    rows = seq_len // GRID_W
    inv = ROPE_THETA ** (-jnp.arange(0, AXIS_DIM, 2, dtype=F32) / AXIS_DIM)
    row_ang = jnp.arange(rows, dtype=F32)[:, None] * inv
    col_ang = jnp.arange(GRID_W, dtype=F32)[:, None] * inv

    def expand(fn):
        return jnp.concatenate([jnp.repeat(fn(row_ang), GRID_W, axis=0), jnp.tile(fn(col_ang), (rows, 1))], axis=-1)

    cos, sin = expand(jnp.cos), expand(jnp.sin)
    return jnp.concatenate([cos, cos], axis=-1), jnp.concatenate([-sin, sin], axis=-1)


def _deinterleave_perm():
    half = jnp.concatenate([jnp.arange(0, HEAD_DIM, 2), jnp.arange(1, HEAD_DIM, 2)])
    heads = jnp.arange(N_Q_HEADS + N_KV_HEADS)[:, None] * HEAD_DIM
    return (heads + half[None, :]).reshape(-1)


def kernel(x, p, norm_ffn1, w1_ffn1, w3_ffn1, w2_ffn1, norm_mix, w_in, q_norm, k_norm, conv_w, conv_b,
           lru_wa, lru_ba, lru_wi, lru_bi, lru_lambda, w_out, norm_ffn2, w1_ffn2, w3_ffn2, w2_ffn2,
           norm_ple, w_ple_gate, w_ple_proj, norm_final):
    b, s, d = x.shape
    depth = w_in.shape[0]
    t = b * s
    cos, sin = _rope_tables(s)
    perm = _deinterleave_perm()
    half = perm[:HEAD_DIM]
    qk_w = ATTN_W + KV_W

    x = x.reshape(t, d)
    for l in range(depth):
        row = lambda v: v.reshape(1, -1)
        x = _ffn(x, row(norm_ffn1[l]), w1_ffn1[l].astype(BF16), w3_ffn1[l].astype(BF16), w2_ffn1[l].astype(BF16))

        w_in_l = w_in[l].astype(BF16)
        w_in_l = jnp.concatenate([w_in_l[:, :qk_w][:, perm], w_in_l[:, qk_w:]], axis=1)
        qt, k, vt, u, y = _inproj(x.reshape(b, s, d), row(norm_mix[l]), w_in_l,
                                  row(q_norm[l][half]), row(k_norm[l][half]), cos, sin)
        attn = _attn(qt, k, vt)

        hc = (0.5 * LRU_C) * jax.nn.softplus(-lru_lambda[l])
        wg = (0.5 * jnp.concatenate([lru_wa[l], lru_wi[l]], axis=-1)).astype(BF16)
        bg = 0.5 * jnp.concatenate([lru_ba[l], lru_bi[l]], axis=-1)[:, :, None, :]
        lru = _lru(u, y, conv_w[l], row(conv_b[l]), wg, bg, hc)

        w_out_l = w_out[l].astype(BF16)
        x = _outproj(x, attn.reshape(t, ATTN_W), lru.reshape(t, LRU_W), w_out_l[:ATTN_W], w_out_l[ATTN_W:])

        x = _ffn(x, row(norm_ffn2[l]), w1_ffn2[l].astype(BF16), w3_ffn2[l].astype(BF16), w2_ffn2[l].astype(BF16))

        x = _ple(x, p[l].reshape(t, -1), row(norm_ple[l]), w_ple_gate[l].astype(BF16),
                 w_ple_proj[l].astype(BF16), row(norm_final), final=(l == depth - 1))
    return x.reshape(b, s, d)
```

```python
import functools
import math

import jax
import jax.numpy as jnp
from jax import lax
from jax.experimental import pallas as pl
from jax.experimental.pallas import tpu as pltpu

F32 = jnp.float32
BF16 = jnp.bfloat16

EPS = 1e-6
HEAD_DIM = 128
N_Q_HEADS = 8
N_KV_HEADS = 2
GROUP = N_Q_HEADS // N_KV_HEADS
ATTN_W = N_Q_HEADS * HEAD_DIM
KV_W = N_KV_HEADS * HEAD_DIM
LRU_BLOCKS = 8
LRU_BW = 128
LRU_W = LRU_BLOCKS * LRU_BW
LRU_C = 8.0
CONV_W = 4
CONV_LEFT = 2
GRID_W = 64
ROPE_THETA = 10000.0
AXIS_DIM = HEAD_DIM // 2

V7X_VMEM_BYTES = 64 * 1024 * 1024
VMEM_LIMIT_BYTES = V7X_VMEM_BYTES - 4 * 1024 * 1024
SUBLANES = 8
LOG2_E = math.log2(math.e)
Q_SCALE = LOG2_E / math.sqrt(HEAD_DIM)
NEG_BIG = -1e30
SAFE_SHIFT_LOG2 = 60.0
FFN_PARTS = 2
FFN_TOKEN_TILE = 512
FFN_COL_CHUNK = 1024
TOKEN_TILE = 512
ATTN_Q_TILE = 512
ATTN_KEY_BLOCK = 1024
KV_UNROLL = 16
SCAN_UNROLL = 8


def _tile(n, pref):
    t = pref
    while n % t:
        t //= 2
    return t


def _params(sem):
    return pltpu.CompilerParams(dimension_semantics=sem, vmem_limit_bytes=VMEM_LIMIT_BYTES)


def _rms(x, g):
    return x * lax.rsqrt(jnp.mean(x * x, axis=-1, keepdims=True) + EPS) * g


def _resident(shape):
    nd = len(shape)
    return pl.BlockSpec(shape, lambda *_: (0,) * nd, pipeline_mode=pl.Buffered(1))


def _ffn_part_body(*refs, first, col_chunk):
    if first:
        y_ref, g_ref, w1_ref, w3_ref, w2_ref, o_ref, h_ref = refs
        h = _rms(y_ref[...], g_ref[...]).astype(BF16)
        h_ref[...] = h
    else:
        h_ref, y_ref, w1_ref, w3_ref, w2_ref, o_ref = refs
        h = h_ref[...]
    o_ref[...] = y_ref[...]
    width = w1_ref.shape[1]
    for c0 in range(0, width, col_chunk):
        cols = slice(c0, min(c0 + col_chunk, width))
        a = jnp.dot(h, w1_ref[:, cols], preferred_element_type=F32)
        b = jnp.dot(h, w3_ref[:, cols], preferred_element_type=F32)
        z = (0.5 * a * jax.nn.sigmoid(a) * b).astype(BF16)
        o_ref[...] += jnp.dot(z, w2_ref[cols, :], preferred_element_type=F32)


def _ffn(x, g, w1, w3, w2):
    t, d = x.shape
    width = w1.shape[1] // FFN_PARTS
    tm = _tile(t, FFN_TOKEN_TILE)
    tile = pl.BlockSpec((tm, d), lambda i: (i, 0))
    y, h = x, None
    for part in range(FFN_PARTS):
        first = part == 0
        weights = [
            pl.BlockSpec((d, width), lambda i, part=part: (0, part), pipeline_mode=pl.Buffered(1)),
            pl.BlockSpec((d, width), lambda i, part=part: (0, part), pipeline_mode=pl.Buffered(1)),
            pl.BlockSpec((width, d), lambda i, part=part: (part, 0), pipeline_mode=pl.Buffered(1)),
        ]
        res = jax.ShapeDtypeStruct((t, d), F32)
        out = pl.pallas_call(
            functools.partial(_ffn_part_body, first=first, col_chunk=FFN_COL_CHUNK),
            out_shape=(res, jax.ShapeDtypeStruct((t, d), BF16)) if first else res,
            grid=(t // tm,),
            in_specs=([tile, pl.BlockSpec((1, d), lambda i: (0, 0))] if first else [tile, tile]) + weights,
            out_specs=(tile, tile) if first else tile,
            compiler_params=_params(("parallel",)),
            name="ffn",
        )(*((y, g) if first else (h, y)), w1, w3, w2)
        y, h = out if first else (out, h)
    return y


def _inproj_body(x_ref, g_ref, w_ref, gq_ref, gk_ref, cos_ref, sin_ref,
                 qt_ref, k_ref, vt_ref, u_ref, y_ref):
    h = _rms(x_ref[0], g_ref[...]).astype(BF16)
    proj = jnp.dot(h, w_ref[...], preferred_element_type=F32)
    cos = jnp.concatenate([cos_ref[...], cos_ref[...]], axis=1)
    sin = jnp.concatenate([-sin_ref[...], sin_ref[...]], axis=1)

    def norm_rope(z, g):
        z = _rms(z, g)
        return z * cos + pltpu.roll(z, HEAD_DIM // 2, 1) * sin

    for hd in range(N_Q_HEADS):
        q = norm_rope(proj[:, hd * HEAD_DIM:(hd + 1) * HEAD_DIM], gq_ref[...]) * Q_SCALE
        qt_ref[0, hd] = q.T.astype(BF16)
    for kv in range(N_KV_HEADS):
        c0 = ATTN_W + kv * HEAD_DIM
        k_ref[0, kv] = norm_rope(proj[:, c0:c0 + HEAD_DIM], gk_ref[...]).astype(BF16)
        c0 = ATTN_W + KV_W + kv * HEAD_DIM
        vt_ref[0, kv] = proj[:, c0:c0 + HEAD_DIM].T.astype(BF16)
    c0 = ATTN_W + 2 * KV_W
    u_ref[0] = proj[:, c0:c0 + LRU_W]
    y_ref[0] = proj[:, c0 + LRU_W:c0 + 2 * LRU_W]


def _inproj(x, g, w, gq, gk, cos, sin):
    b, s, d = x.shape
    tm = _tile(s, TOKEN_TILE)
    n = s // tm
    return pl.pallas_call(
        _inproj_body,
        out_shape=(
            jax.ShapeDtypeStruct((b, N_Q_HEADS, HEAD_DIM, s), BF16),
            jax.ShapeDtypeStruct((b, N_KV_HEADS, s, HEAD_DIM), BF16),
            jax.ShapeDtypeStruct((b, N_KV_HEADS, HEAD_DIM, s), BF16),
            jax.ShapeDtypeStruct((b, s, LRU_W), F32),
            jax.ShapeDtypeStruct((b, s, LRU_W), F32),
        ),
        grid=(b, n),
        in_specs=[
            pl.BlockSpec((1, tm, d), lambda bi, i: (bi, i, 0)),
            pl.BlockSpec((1, d), lambda bi, i: (0, 0)),
            _resident(w.shape),
            pl.BlockSpec((1, HEAD_DIM), lambda bi, i: (0, 0)),
            pl.BlockSpec((1, HEAD_DIM), lambda bi, i: (0, 0)),
            pl.BlockSpec((tm, HEAD_DIM // 2), lambda bi, i: (i, 0)),
            pl.BlockSpec((tm, HEAD_DIM // 2), lambda bi, i: (i, 0)),
        ],
        out_specs=(
            pl.BlockSpec((1, N_Q_HEADS, HEAD_DIM, tm), lambda bi, i: (bi, 0, 0, i)),
            pl.BlockSpec((1, N_KV_HEADS, tm, HEAD_DIM), lambda bi, i: (bi, 0, i, 0)),
            pl.BlockSpec((1, N_KV_HEADS, HEAD_DIM, tm), lambda bi, i: (bi, 0, 0, i)),
            pl.BlockSpec((1, tm, LRU_W), lambda bi, i: (bi, i, 0)),
            pl.BlockSpec((1, tm, LRU_W), lambda bi, i: (bi, i, 0)),
        ),
        compiler_params=_params(("parallel", "parallel")),
        name="inproj",
    )(x, g, w, gq, gk, cos, sin)


def _attn_body(qt_ref, k_ref, vt_ref, o_ref, kmax_ref, m_ref, l_ref, acc_ref, s0_ref, s1_ref, *, tk):
    s_len = k_ref.shape[2]
    n_kv = s_len // tk

    @pl.when(pl.program_id(2) == 0)
    def _():
        def key_chunk(j, best):
            kc = k_ref[0, 0, pl.ds(pl.multiple_of(j * tk, tk), tk), :].astype(F32)
            return jnp.maximum(best, jnp.sum(kc * kc, axis=1, keepdims=True))
        best = lax.fori_loop(0, n_kv, key_chunk, jnp.zeros((tk, 1), F32))
        kmax_ref[...] = jnp.max(best, axis=0, keepdims=True)

    for g in range(GROUP):
        qf = qt_ref[0, g].astype(F32)
        m_ref[g] = jnp.sqrt(jnp.sum(qf * qf, axis=0, keepdims=True) * kmax_ref[...])
    l_ref[...] = jnp.zeros(l_ref.shape, F32)
    acc_ref[...] = jnp.zeros(acc_ref.shape, F32)
    shift_is_safe = jnp.max(m_ref[...]) <= SAFE_SHIFT_LOG2

    def key_block(j):
        return k_ref[0, 0, pl.ds(pl.multiple_of(j * tk, tk), tk), :]

    def value_block(j):
        return vt_ref[0, 0, :, pl.ds(pl.multiple_of(j * tk, tk), tk)]

    def scores(j, s_ref):
        kb = key_block(j)
        for g in range(GROUP):
            s_ref[g] = jnp.dot(kb, qt_ref[0, g], preferred_element_type=F32)

    def accumulate(j, s_ref):
        vb = value_block(j)
        for g in range(GROUP):
            p = jnp.exp2(s_ref[g] - m_ref[g])
            l_ref[g] += jnp.sum(p, axis=0, keepdims=True)
            acc_ref[g] += jnp.dot(vb, p.astype(BF16), preferred_element_type=F32)

    def online_step(j, carry):
        kb, vb = key_block(j), value_block(j)
        for g in range(GROUP):
            s = jnp.dot(kb, qt_ref[0, g], preferred_element_type=F32)
            m_old = m_ref[g]
            m_new = jnp.maximum(m_old, jnp.max(s, axis=0, keepdims=True))
            alpha = jnp.exp2(m_old - m_new)
            p = jnp.exp2(s - m_new)
            l_ref[g] = alpha * l_ref[g] + jnp.sum(p, axis=0, keepdims=True)
            acc_ref[g] = alpha * acc_ref[g] + jnp.dot(vb, p.astype(BF16), preferred_element_type=F32)
            m_ref[g] = m_new
        return carry

    @pl.when(shift_is_safe)
    def _():
        bufs = (s0_ref, s1_ref)
        scores(0, s0_ref)

        def group(t, carry):
            for u in range(KV_UNROLL):
                j = KV_UNROLL * t + u
                scores(j + 1, bufs[(u + 1) % 2])
                accumulate(j, bufs[u % 2])
            return carry

        lax.fori_loop(0, n_kv // KV_UNROLL - 1, group, 0)
        for u in range(KV_UNROLL):
            j = n_kv - KV_UNROLL + u
            if u + 1 < KV_UNROLL:
                scores(j + 1, bufs[(u + 1) % 2])
            accumulate(j, bufs[u % 2])

    @pl.when(jnp.logical_not(shift_is_safe))
    def _():
        m_ref[...] = jnp.full(m_ref.shape, NEG_BIG, F32)
        lax.fori_loop(0, n_kv, online_step, 0)

    for g in range(GROUP):
        o = acc_ref[g] / l_ref[g]
        o_ref[0, :, g * HEAD_DIM:(g + 1) * HEAD_DIM] = o.T.astype(o_ref.dtype)


def _attn(qt, k, vt):
    b, _, _, s = qt.shape
    tq = _tile(s, ATTN_Q_TILE)
    tk = _tile(s // KV_UNROLL, ATTN_KEY_BLOCK)
    assert (s // tk) % KV_UNROLL == 0
    return pl.pallas_call(
        functools.partial(_attn_body, tk=tk),
        out_shape=jax.ShapeDtypeStruct((b, s, ATTN_W), BF16),
        grid=(b, N_KV_HEADS, s // tq),
        in_specs=[
            pl.BlockSpec((1, GROUP, HEAD_DIM, tq), lambda bi, kv, i: (bi, kv, 0, i)),
            pl.BlockSpec((1, 1, s, HEAD_DIM), lambda bi, kv, i: (bi, kv, 0, 0)),
            pl.BlockSpec((1, 1, HEAD_DIM, s), lambda bi, kv, i: (bi, kv, 0, 0)),
        ],
        out_specs=pl.BlockSpec((1, tq, GROUP * HEAD_DIM), lambda bi, kv, i: (bi, i, kv)),
        scratch_shapes=[
            pltpu.VMEM((1, 1), F32),
            pltpu.VMEM((GROUP, 1, tq), F32),
            pltpu.VMEM((GROUP, 1, tq), F32),
            pltpu.VMEM((GROUP, HEAD_DIM, tq), F32),
            pltpu.VMEM((GROUP, tk, tq), F32),
            pltpu.VMEM((GROUP, tk, tq), F32),
        ],
        compiler_params=_params(("parallel", "parallel", "arbitrary")),
        name="attn",
    )(qt, k, vt)


def _gelu_tanh(y):
    return 0.5 * y * (1.0 + jnp.tanh(math.sqrt(2.0 / math.pi) * (y + 0.044715 * (y * y * y))))


def _seg_pitch(ts):
    return ts // SUBLANES + SUBLANES // 2


def _lru_scan(uc, wg_ref, bg_ref, hc_ref, a_ref, b_ref, carry_ref, *, reverse):
    ts = uc.shape[0]
    seg = ts // SUBLANES
    pitch = _seg_pitch(ts)

    @pl.when(pl.program_id(1) == 0)
    def _():
        carry_ref[...] = jnp.zeros(carry_ref.shape, F32)

    ucb = uc.astype(BF16)
    for hb in range(LRU_BLOCKS):
        cols = slice(hb * LRU_BW, (hb + 1) * LRU_BW)
        half = jnp.dot(ucb[:, cols], wg_ref[hb], preferred_element_type=F32) + bg_ref[hb]
        t_r = jnp.tanh(half[:, :LRU_BW])
        t_i = jnp.tanh(half[:, LRU_BW:])
        n = hc_ref[:, cols] * t_r + hc_ref[:, cols]
        a = jnp.exp2(n * (-LOG2_E))
        b = jnp.sqrt(jnp.tanh(n) * (1.0 + a * a)) * (0.5 * t_i + 0.5) * uc[:, cols]
        for sg in range(SUBLANES):
            a_ref[hb, pitch * sg:pitch * sg + seg, :] = a[seg * sg:seg * (sg + 1), :]
            b_ref[hb, pitch * sg:pitch * sg + seg, :] = b[seg * sg:seg * (sg + 1), :]

    unroll = min(SCAN_UNROLL, seg)

    def sweep(step, state):
        def trip(_, carried):
            base, st = carried
            for k in range(unroll):
                st = step(pl.ds(base + (unroll - 1 - k if reverse else k), SUBLANES, stride=pitch), st)
            return base + (-unroll if reverse else unroll), st
        first = jnp.int32(seg - unroll if reverse else 0)
        return lax.fori_loop(0, seg // unroll, trip, (first, state))[1]

    def local_step(rows, state):
        hs, ps = state
        new_h, new_p = [], []
        for hb in range(LRU_BLOCKS):
            a = a_ref[hb, rows, :]
            new_h.append(a * hs[hb] + b_ref[hb, rows, :])
            new_p.append(a * ps[hb])
        return tuple(new_h), tuple(new_p)

    zeros = tuple(jnp.zeros((SUBLANES, LRU_BW), F32) for _ in range(LRU_BLOCKS))
    ones = tuple(jnp.ones((SUBLANES, LRU_BW), F32) for _ in range(LRU_BLOCKS))
    h_end, p_end = sweep(local_step, (zeros, ones))

    row = lax.broadcasted_iota(jnp.int32, (SUBLANES, LRU_BW), 0)
    starts = []
    for hb in range(LRU_BLOCKS):
        cols = slice(hb * LRU_BW, (hb + 1) * LRU_BW)
        a, b = p_end[hb], h_end[hb]
        for dist in (1, 2, 4):
            if reverse:
                shift, valid = SUBLANES - dist, row < SUBLANES - dist
            else:
                shift, valid = dist, row >= dist
            a_s = jnp.where(valid, pltpu.roll(a, shift, 0), 1.0)
            b_s = jnp.where(valid, pltpu.roll(b, shift, 0), 0.0)
            b = a * b_s + b
            a = a * a_s
        carry = carry_ref[:, cols]
        ends = a * carry + b
        if reverse:
            starts.append(jnp.where(row == SUBLANES - 1, carry, pltpu.roll(ends, SUBLANES - 1, 0)))
            carry_ref[:, cols] = ends[0:1, :]
        else:
            starts.append(jnp.where(row == 0, carry, pltpu.roll(ends, 1, 0)))
            carry_ref[:, cols] = ends[SUBLANES - 1:SUBLANES, :]

    def final_step(rows, hs):
        new_h = []
        for hb in range(LRU_BLOCKS):
            h = a_ref[hb, rows, :] * hs[hb] + b_ref[hb, rows, :]
            b_ref[hb, rows, :] = h
            new_h.append(h)
        return tuple(new_h)

    sweep(final_step, tuple(starts))
    return jnp.concatenate(
        [jnp.concatenate([b_ref[hb, pitch * sg:pitch * sg + seg, :] for sg in range(SUBLANES)], axis=0)
         for hb in range(LRU_BLOCKS)], axis=1)


def _lru_fwd_body(u_ref, up_ref, un_ref, cw_ref, cb_ref, wg_ref, bg_ref, hc_ref,
                  h_ref, uc_ref, ext_ref, a_ref, b_ref, carry_ref):
    ts = u_ref.shape[1]
    i = pl.program_id(1)
    ext_ref[0:SUBLANES, :] = jnp.where(i > 0, up_ref[0], 0.0)
    ext_ref[SUBLANES:SUBLANES + ts, :] = u_ref[0]
    ext_ref[SUBLANES + ts:2 * SUBLANES + ts, :] = jnp.where(i < pl.num_programs(1) - 1, un_ref[0], 0.0)
    ext = ext_ref[...]
    uc = cb_ref[...]
    for j in range(CONV_W):
        shift = (CONV_LEFT - j) % ext.shape[0]
        tap = pltpu.roll(ext, shift, 0) if shift else ext
        uc = uc + tap[SUBLANES:SUBLANES + ts, :] * cw_ref[j:j + 1, :]
    uc_ref[0] = uc
    h_ref[0] = _lru_scan(uc, wg_ref, bg_ref, hc_ref, a_ref, b_ref, carry_ref, reverse=False)


def _lru_bwd_body(uc_ref, wg_ref, bg_ref, hc_ref, hf_ref, y_ref, o_ref, a_ref, b_ref, carry_ref):
    h_b = _lru_scan(uc_ref[0], wg_ref, bg_ref, hc_ref, a_ref, b_ref, carry_ref, reverse=True)
    o_ref[0] = ((hf_ref[0] + h_b) * _gelu_tanh(y_ref[0])).astype(o_ref.dtype)


def _lru(u, y, cw, cb, wg, bg, hc):
    b, s, w = u.shape
    ts = _tile(s, TOKEN_TILE)
    n = s // ts
    nb8 = ts // SUBLANES

    def whole(a):
        return pl.BlockSpec(a.shape, lambda bi, i: (0,) * a.ndim)

    fwd_tile = pl.BlockSpec((1, ts, w), lambda bi, i: (bi, i, 0))
    bwd_tile = pl.BlockSpec((1, ts, w), lambda bi, i: (bi, n - 1 - i, 0))
    seg_rows = SUBLANES * _seg_pitch(ts)
    scan_scratch = [pltpu.VMEM((LRU_BLOCKS, seg_rows, LRU_BW), F32), pltpu.VMEM((LRU_BLOCKS, seg_rows, LRU_BW), F32),
                    pltpu.VMEM((1, w), F32)]
    h_f, uc = pl.pallas_call(
        _lru_fwd_body,
        out_shape=(jax.ShapeDtypeStruct((b, s, w), F32), jax.ShapeDtypeStruct((b, s, w), F32)),
        grid=(b, n),
        in_specs=[
            fwd_tile,
            pl.BlockSpec((1, SUBLANES, w), lambda bi, i: (bi, jnp.maximum(i * nb8 - 1, 0), 0)),
            pl.BlockSpec((1, SUBLANES, w), lambda bi, i: (bi, jnp.minimum((i + 1) * nb8, s // SUBLANES - 1), 0)),
            whole(cw), whole(cb), whole(wg[0]), whole(bg[0]), whole(hc[0:1]),
        ],
        out_specs=(fwd_tile, fwd_tile),
        scratch_shapes=[pltpu.VMEM((ts + 2 * SUBLANES, w), F32)] + scan_scratch,
        compiler_params=_params(("parallel", "arbitrary")),
        name="lru_fwd",
    )(u, u, u, cw, cb, wg[0], bg[0], hc[0:1])
    return pl.pallas_call(
        _lru_bwd_body,
        out_shape=jax.ShapeDtypeStruct((b, s, w), BF16),
        grid=(b, n),
        in_specs=[bwd_tile, whole(wg[1]), whole(bg[1]), whole(hc[1:2]), bwd_tile, bwd_tile],
        out_specs=bwd_tile,
        scratch_shapes=scan_scratch,
        compiler_params=_params(("parallel", "arbitrary")),
        name="lru_bwd",
    )(uc, wg[1], bg[1], hc[1:2], h_f, y)


def _outproj_body(x_ref, a_ref, r_ref, wa_ref, wr_ref, o_ref):
    o_ref[...] = (x_ref[...]
                  + jnp.dot(a_ref[...], wa_ref[...], preferred_element_type=F32)
                  + jnp.dot(r_ref[...], wr_ref[...], preferred_element_type=F32))


def _outproj(x, attn, lru, wa, wr):
    t, d = x.shape
    tm = _tile(t, TOKEN_TILE)
    return pl.pallas_call(
        _outproj_body,
        out_shape=jax.ShapeDtypeStruct((t, d), F32),
        grid=(t // tm,),
        in_specs=[
            pl.BlockSpec((tm, d), lambda i: (i, 0)),
            pl.BlockSpec((tm, attn.shape[1]), lambda i: (i, 0)),
            pl.BlockSpec((tm, lru.shape[1]), lambda i: (i, 0)),
            _resident(wa.shape),
            _resident(wr.shape),
        ],
        out_specs=pl.BlockSpec((tm, d), lambda i: (i, 0)),
        compiler_params=_params(("parallel",)),
        name="outproj",
    )(x, attn, lru, wa, wr)


def _ple_body(x_ref, p_ref, g_ref, wg_ref, wp_ref, gf_ref, o_ref, *, final):
    x = x_ref[...]
    h = _rms(x, g_ref[...]).astype(BF16)
    gate = jax.nn.sigmoid(jnp.dot(h, wg_ref[...], preferred_element_type=F32))
    emb = jnp.dot(p_ref[...].astype(BF16), wp_ref[...], preferred_element_type=F32)
    x = x + gate * emb
    o_ref[...] = _rms(x, gf_ref[...]) if final else x


def _ple(x, p, g, wg, wp, gf, *, final):
    t, d = x.shape
    tm = _tile(t, TOKEN_TILE)
    return pl.pallas_call(
        functools.partial(_ple_body, final=final),
        out_shape=jax.ShapeDtypeStruct((t, d), F32),
        grid=(t // tm,),
        in_specs=[
            pl.BlockSpec((tm, d), lambda i: (i, 0)),
            pl.BlockSpec((tm, p.shape[1]), lambda i: (i, 0)),
            pl.BlockSpec((1, d), lambda i: (0, 0)),
            _resident(wg.shape),
            _resident(wp.shape),
            pl.BlockSpec((1, d), lambda i: (0, 0)),
        ],
        out_specs=pl.BlockSpec((tm, d), lambda i: (i, 0)),
        compiler_params=_params(("parallel",)),
        name="ple",
    )(x, p, g, wg, wp, gf)


def _rope_tables(seq_len):
    rows = seq_len // GRID_W
    inv = ROPE_THETA ** (-jnp.arange(0, AXIS_DIM, 2, dtype=F32) / AXIS_DIM)
    row_ang = jnp.arange(rows, dtype=F32)[:, None] * inv
    col_ang = jnp.arange(GRID_W, dtype=F32)[:, None] * inv

    def expand(fn):
        return jnp.concatenate([jnp.repeat(fn(row_ang), GRID_W, axis=0), jnp.tile(fn(col_ang), (rows, 1))], axis=-1)

    return expand(jnp.cos), expand(jnp.sin)


def _deinterleave_perm():
    half = jnp.concatenate([jnp.arange(0, HEAD_DIM, 2), jnp.arange(1, HEAD_DIM, 2)])
    heads = jnp.arange(N_Q_HEADS + N_KV_HEADS)[:, None] * HEAD_DIM
    return (heads + half[None, :]).reshape(-1)


def kernel(x, p, norm_ffn1, w1_ffn1, w3_ffn1, w2_ffn1, norm_mix, w_in, q_norm, k_norm, conv_w, conv_b,
           lru_wa, lru_ba, lru_wi, lru_bi, lru_lambda, w_out, norm_ffn2, w1_ffn2, w3_ffn2, w2_ffn2,
           norm_ple, w_ple_gate, w_ple_proj, norm_final):
    b, s, d = x.shape
    depth = w_in.shape[0]
    t = b * s
    cos, sin = _rope_tables(s)
    perm = _deinterleave_perm()
    half = perm[:HEAD_DIM]
    qk_w = ATTN_W + KV_W

    x = x.reshape(t, d)
    for l in range(depth):
        row = lambda v: v.reshape(1, -1)
        x = _ffn(x, row(norm_ffn1[l]), w1_ffn1[l].astype(BF16), w3_ffn1[l].astype(BF16), w2_ffn1[l].astype(BF16))

        w_in_l = w_in[l].astype(BF16)
        w_in_l = jnp.concatenate([w_in_l[:, :qk_w][:, perm], w_in_l[:, qk_w:]], axis=1)
        qt, k, vt, u, y = _inproj(x.reshape(b, s, d), row(norm_mix[l]), w_in_l,
                                  row(q_norm[l][half]), row(k_norm[l][half]), cos, sin)
        attn = _attn(qt, k, vt)

        hc = (0.5 * LRU_C) * jax.nn.softplus(-lru_lambda[l])
        wg = (0.5 * jnp.concatenate([lru_wa[l], lru_wi[l]], axis=-1)).astype(BF16)
        bg = 0.5 * jnp.concatenate([lru_ba[l], lru_bi[l]], axis=-1)[:, :, None, :]
        lru = _lru(u, y, conv_w[l], row(conv_b[l]), wg, bg, hc)

        w_out_l = w_out[l].astype(BF16)
        x = _outproj(x, attn.reshape(t, ATTN_W), lru.reshape(t, LRU_W), w_out_l[:ATTN_W], w_out_l[ATTN_W:])

        x = _ffn(x, row(norm_ffn2[l]), w1_ffn2[l].astype(BF16), w3_ffn2[l].astype(BF16), w2_ffn2[l].astype(BF16))

        x = _ple(x, p[l].reshape(t, -1), row(norm_ple[l]), w_ple_gate[l].astype(BF16),
                 w_ple_proj[l].astype(BF16), row(norm_final), final=(l == depth - 1))
    return x.reshape(b, s, d)
```
